```python
import jax, jax.numpy as jnp
from jax import lax
import numpy as np

D_MODEL = 1024
BATCH = 16
SEQ = 2048
DEPTH = 1

POOL_WIDTH = D_MODEL // 2
POOL_WINDOWS = (2, 4, 8, 16)
N_POOL_GROUPS = len(POOL_WINDOWS)
POOL_GROUP = POOL_WIDTH // N_POOL_GROUPS
N_HEADS = 8
HEAD_DIM = 64
ATTN_WIDTH = N_HEADS * HEAD_DIM
MOBA_BLOCK = 256
MOBA_TOPK = 3
Q_CHUNK = 32
D_FF = 4 * D_MODEL
N_BRANCHES = 2
IN_WIDTH = POOL_WIDTH + 3 * ATTN_WIDTH + N_BRANCHES * D_MODEL
EPS = 1e-6
NEG = -1e30

kernel_name = "hybrid_pool_moba_gated_block"


def rmsnorm(x, g):
    xf = x.astype(jnp.float32)
    y = xf * lax.rsqrt(jnp.mean(xf * xf, axis=-1, keepdims=True) + EPS) * g.astype(jnp.float32)
    return y.astype(x.dtype)


def alibi_slopes(n_heads):
    return jnp.asarray(2.0 ** (-8.0 * np.arange(1, n_heads + 1) / n_heads), dtype=jnp.float32)


def pool_mixer(p, pool_w, pool_scale):
    pf = p.astype(jnp.float32)
    B, S, _ = pf.shape
    cs = jnp.concatenate([jnp.zeros_like(pf[:, :1]), jnp.cumsum(pf, axis=1)], axis=1)
    t = jnp.arange(S)
    outs = []
    for g, w in enumerate(POOL_WINDOWS):
        sl = slice(g * POOL_GROUP, (g + 1) * POOL_GROUP)
        lo = jnp.maximum(t + 1 - w, 0)
        win_sum = cs[:, 1:, sl] - cs[:, lo, sl]
        count = jnp.minimum(t + 1, w).astype(jnp.float32)[None, :, None]
        outs.append(win_sum / count - pf[:, :, sl])
    y = jnp.stack(outs, axis=2)
    y = jnp.einsum('bsgc,gcd->bsgd', y, pool_w.astype(jnp.float32))
    y = y.reshape(B, S, POOL_WIDTH) * pool_scale.astype(jnp.float32)
    return y.astype(p.dtype)


def moba_attention(q, k, v):
    B, S, H, Dh = q.shape
    nb = -(-S // MOBA_BLOCK)
    s_pad = nb * MOBA_BLOCK
    qf = q.transpose(0, 2, 1, 3).astype(jnp.float32) * (HEAD_DIM ** -0.5)
    pad = ((0, 0), (0, 0), (0, s_pad - S), (0, 0))
    kf = jnp.pad(k.transpose(0, 2, 1, 3).astype(jnp.float32), pad)
    vf = jnp.pad(v.transpose(0, 2, 1, 3).astype(jnp.float32), pad)
    kb = kf.reshape(B, H, nb, MOBA_BLOCK, Dh)
    vb = vf.reshape(B, H, nb, MOBA_BLOCK, Dh)
    slope = alibi_slopes(H)[None, :, None, None]
    t = jnp.arange(S)
    qblk = t // MOBA_BLOCK
    k_sel = min(MOBA_TOPK, nb - 1)
    if k_sel > 0:
        kmean = jnp.mean(kb, axis=3)
        gate = jnp.einsum('bhsd,bhnd->bhsn', qf, kmean)
        past = jnp.arange(nb)[None, :] < qblk[:, None]
        gate = jnp.where(past[None, None], gate, NEG)
        _, sel = lax.top_k(gate, k_sel)
    else:
        sel = jnp.zeros((B, H, S, 0), dtype=jnp.int32)
    n_chunks = S // Q_CHUNK
    q_c = qf.reshape(B, H, n_chunks, Q_CHUNK, Dh).transpose(2, 0, 1, 3, 4)
    sel_c = sel.reshape(B, H, n_chunks, Q_CHUNK, k_sel).transpose(2, 0, 1, 3, 4)
    bi = jnp.arange(B)[:, None, None]
    hi = jnp.arange(H)[None, :, None]
    kpos_in = jnp.arange(MOBA_BLOCK)

    def chunk_attn(args):
        c, qc, selc = args
        tq = c * Q_CHUNK + jnp.arange(Q_CHUNK)
        own = (c * Q_CHUNK) // MOBA_BLOCK
        k_own = lax.dynamic_slice_in_dim(kf, own * MOBA_BLOCK, MOBA_BLOCK, axis=2)
        v_own = lax.dynamic_slice_in_dim(vf, own * MOBA_BLOCK, MOBA_BLOCK, axis=2)
        ks = own * MOBA_BLOCK + kpos_in
        dist = (tq[:, None] - ks[None, :]).astype(jnp.float32)
        s_own = jnp.einsum('bhqd,bhkd->bhqk', qc, k_own) - slope * dist[None, None]
        s_own = jnp.where((ks[None, :] <= tq[:, None])[None, None], s_own, NEG)
        scores = [s_own]
        idxs = []
        for j in range(k_sel):
            idx = selc[..., j]
            kg = kb[bi, hi, idx]
            kpos = idx[..., None] * MOBA_BLOCK + kpos_in
            d = (tq[None, None, :, None] - kpos).astype(jnp.float32)
            s = jnp.einsum('bhqd,bhqkd->bhqk', qc, kg) - slope * d
            s = jnp.where((idx < own)[..., None], s, NEG)
            scores.append(s)
            idxs.append(idx)
        p = jax.nn.softmax(jnp.concatenate(scores, axis=-1), axis=-1)
        out = jnp.einsum('bhqk,bhkd->bhqd', p[..., :MOBA_BLOCK], v_own)
        for j, idx in enumerate(idxs):
            vg = vb[bi, hi, idx]
            pj = p[..., (j + 1) * MOBA_BLOCK:(j + 2) * MOBA_BLOCK]
            out = out + jnp.einsum('bhqk,bhqkd->bhqd', pj, vg)
        return out

    out = lax.map(chunk_attn, (jnp.arange(n_chunks), q_c, sel_c))
    out = out.transpose(1, 0, 3, 2, 4).reshape(B, S, H * Dh)
    return out.astype(q.dtype)


def setup_inputs(seed: int = 0) -> dict:
    key = jax.random.key(seed)
    ks = jax.random.split(key, 16)
    f32 = jnp.float32

    def nrm(k, shape, fan_in):
        return jax.random.normal(k, shape, f32) * (fan_in ** -0.5)

    def gain(k, shape):
        return 1.0 + 0.02 * jax.random.normal(k, shape, f32)

    L = DEPTH
    return {
        "x": jax.random.normal(ks[0], (BATCH, SEQ, D_MODEL), f32),
        "norm_mix_pre": gain(ks[1], (L, D_MODEL)),
        "w_in": nrm(ks[2], (L, D_MODEL, IN_WIDTH), D_MODEL),
        "b_gate": 0.01 * jax.random.normal(ks[3], (L, N_BRANCHES * D_MODEL), f32),
        "pool_w": nrm(ks[4], (L, N_POOL_GROUPS, POOL_GROUP, POOL_GROUP), POOL_GROUP),
        "pool_scale": gain(ks[5], (L, POOL_WIDTH)),
        "w_branch_pool": nrm(ks[6], (L, POOL_WIDTH, D_MODEL), POOL_WIDTH),
        "w_branch_attn": nrm(ks[7], (L, ATTN_WIDTH, D_MODEL), ATTN_WIDTH),
        "w_out": nrm(ks[8], (L, D_MODEL, D_MODEL), D_MODEL),
        "norm_mix_post": gain(ks[9], (L, D_MODEL)),
        "norm_mlp_pre": gain(ks[10], (L, D_MODEL)),
        "w_up": nrm(ks[11], (L, D_MODEL, D_FF), D_MODEL),
        "w_down": nrm(ks[12], (L, D_FF, D_MODEL), D_FF),
        "norm_mlp_post": gain(ks[13], (L, D_MODEL)),
    }


def reference(x, norm_mix_pre, w_in, b_gate, pool_w, pool_scale, w_branch_pool, w_branch_attn, w_out,
              norm_mix_post, norm_mlp_pre, w_up, w_down, norm_mlp_post):
    B, S, _ = x.shape
    h = x
    o_q = POOL_WIDTH
    o_k = o_q + ATTN_WIDTH
    o_v = o_k + ATTN_WIDTH
    o_g = o_v + ATTN_WIDTH
    for l in range(DEPTH):
        u = rmsnorm(h, norm_mix_pre[l])
        z = u @ w_in[l]
        p = z[..., :o_q]
        q = z[..., o_q:o_k].reshape(B, S, N_HEADS, HEAD_DIM)
        k = z[..., o_k:o_v].reshape(B, S, N_HEADS, HEAD_DIM)
        v = z[..., o_v:o_g].reshape(B, S, N_HEADS, HEAD_DIM)
        g = z[..., o_g:] + b_gate[l]
        g_pool, g_attn = g[..., :D_MODEL], g[..., D_MODEL:]
        y_pool = pool_mixer(p, pool_w[l], pool_scale[l]) @ w_branch_pool[l]
        y_attn = moba_attention(q, k, v) @ w_branch_attn[l]
        m = jax.nn.sigmoid(g_pool) * y_pool + jax.nn.sigmoid(g_attn) * y_attn
        h = h + rmsnorm(m @ w_out[l], norm_mix_post[l])
        u2 = rmsnorm(h, norm_mlp_pre[l])
        a = jnp.square(jax.nn.relu(u2 @ w_up[l]))
        h = h + rmsnorm(a @ w_down[l], norm_mlp_post[l])
    return h
```

```python
import functools

import jax
import jax.numpy as jnp
import numpy as np
from jax import lax
from jax.experimental import pallas as pl
from jax.experimental.pallas import tpu as pltpu

D_MODEL = 1024
POOL_WIDTH = 512
POOL_WINDOWS = (2, 4, 8, 16)
POOL_GROUP = 128
N_HEADS = 8
HEAD_DIM = 64
ATTN_WIDTH = N_HEADS * HEAD_DIM
MOBA_BLOCK = 256
MOBA_TOPK = 3
D_FF = 4 * D_MODEL
IN_WIDTH = POOL_WIDTH + 3 * ATTN_WIDTH + 2 * D_MODEL
EPS = 1e-6
NEG = -1e30

LANES = 128
HEADS_PER_STEP = LANES // HEAD_DIM
N_FEAT = LANES - HEAD_DIM
HALO = 16
VMEM_LIMIT = 56 * 1024 * 1024

F32 = jnp.float32
BF16 = jnp.bfloat16


def _rms(x, g):
    return x * lax.rsqrt(jnp.mean(x * x, axis=-1, keepdims=True) + EPS) * g


def _const_spec(shape):
    nd = len(shape)
    return pl.BlockSpec(shape, lambda *_: (0,) * nd, pipeline_mode=pl.Buffered(1))


def _in_proj_kernel(x_ref, g_ref, w_ref, b_ref, z_ref, *, col_chunk):
    u = _rms(x_ref[...], g_ref[...]).astype(BF16)
    for c in range(IN_WIDTH // col_chunk):
        cs = slice(c * col_chunk, (c + 1) * col_chunk)
        z = jnp.dot(u, w_ref[:, cs], preferred_element_type=F32) + b_ref[:, cs]
        z_ref[:, cs] = z.astype(BF16)


def _in_proj(x2, g, w_bf, bias, *, tm=512, col_chunk=512):
    T = x2.shape[0]
    return pl.pallas_call(
        functools.partial(_in_proj_kernel, col_chunk=col_chunk),
        out_shape=jax.ShapeDtypeStruct((T, IN_WIDTH), BF16),
        grid=(T // tm,),
        in_specs=[
            pl.BlockSpec((tm, D_MODEL), lambda t: (t, 0)),
            _const_spec((1, D_MODEL)),
            _const_spec((D_MODEL, IN_WIDTH)),
            _const_spec((1, IN_WIDTH)),
        ],
        out_specs=pl.BlockSpec((tm, IN_WIDTH), lambda t: (t, 0)),
        compiler_params=pltpu.CompilerParams(
            dimension_semantics=("parallel",), vmem_limit_bytes=VMEM_LIMIT),
        name="in_proj",
    )(x2, g, w_bf, bias)


def _moba_kernel(q_ref, k_ref, v_ref, kfeat_ref, qconst_ref, o_ref,
                 kaug_ref, qT_ref, vT_ref, s_ref, oT_ref, *, seq):
    nb = seq // MOBA_BLOCK
    blk = MOBA_BLOCK
    lane = lax.broadcasted_iota(jnp.int32, (1, LANES), 1)
    row = lax.broadcasted_iota(jnp.int32, (LANES, 1), 0)
    r8 = lax.broadcasted_iota(jnp.int32, (nb, blk), 0)
    tri = (lax.broadcasted_iota(jnp.int32, (blk, blk), 0)
           <= lax.broadcasted_iota(jnp.int32, (blk, blk), 1))

    k2 = k_ref[0]
    kmean = k2.astype(F32).reshape(nb, blk, LANES).sum(axis=1) * (1.0 / blk)
    for i in range(nb):
        rs = slice(i * blk, (i + 1) * blk)
        qT_ref[:, rs] = q_ref[0, rs, :].astype(F32).T * (HEAD_DIM ** -0.5)

    for hh in range(HEADS_PER_STEP):
        lo = hh * HEAD_DIM
        head_lanes = (lane >= lo) & (lane < lo + HEAD_DIM)
        head_rows = (row >= lo) & (row < lo + HEAD_DIM)
        kaug_ref[...] = jnp.where(head_lanes, k2, kfeat_ref[...])
        kmean_h = jnp.where(head_lanes, kmean, 0.0)
        for j in range(nb):
            rs = slice(j * blk, (j + 1) * blk)
            vT = v_ref[0, rs, :].astype(F32).T
            vT_ref[j] = jnp.where(head_rows, vT, 0.0).astype(BF16)
        qc = qconst_ref[hh]
        slope = qc[0:1, :]

        for i in range(nb):
            qTi = qT_ref[:, i * blk:(i + 1) * blk]
            if i > MOBA_TOPK:
                gate = jnp.dot(kmean_h, qTi, preferred_element_type=F32,
                               precision=lax.Precision.HIGHEST)
                past = r8 < i
            bias_rows = []
            for j in range(nb):
                if j > i:
                    bias_rows.append(jnp.zeros((1, blk), F32))
                    continue
                bj = slope * float(blk * (j - i))
                if j < i and i > MOBA_TOPK:
                    gj = gate[j:j + 1, :]
                    beats = ((gate > gj) | ((gate == gj) & (r8 < j))) & past
                    cnt = jnp.sum(beats.astype(F32), axis=0, keepdims=True)
                    bj = jnp.where(cnt < float(MOBA_TOPK), bj, NEG)
                bias_rows.append(bj)
            feat = jnp.concatenate(bias_rows + [qc], axis=0)
            qh = qTi[lo:lo + HEAD_DIM]
            parts = [qh, feat] if hh == 0 else [feat, qh]
            qaug = jnp.concatenate(parts, axis=0).astype(BF16)

            def scores(j):
                kj = kaug_ref[pl.ds(pl.multiple_of(j * blk, blk), blk), :]
                return jnp.dot(kj, qaug, preferred_element_type=F32)

            def colmax(s):
                return s.reshape(blk // 8, 8, blk).max(axis=0)

            def pass1(j, mx):
                s = scores(j)
                s_ref[j] = s
                return jnp.maximum(mx, colmax(s))

            mx = jnp.full((8, blk), -jnp.inf, F32)
            if i > 0:
                mx = lax.fori_loop(0, i, pass1, mx)
            s = jnp.where(tri, scores(i), NEG)
            s_ref[i] = s
            mx = jnp.maximum(mx, colmax(s))
            m = mx.max(axis=0, keepdims=True)

            def pass2(j, carry):
                l8, acc = carry
                p = jnp.exp(s_ref[j] - m)
                l8 = l8 + p.reshape(blk // 8, 8, blk).sum(axis=0)
                acc = acc + jnp.dot(vT_ref[j], p.astype(BF16), preferred_element_type=F32)
                return l8, acc

            l8, acc = lax.fori_loop(
                0, i + 1, pass2,
                (jnp.zeros((8, blk), F32), jnp.zeros((LANES, blk), F32)))
            out = acc / l8.sum(axis=0, keepdims=True)
            if hh == 0:
                oT_ref[i] = out
            else:
                rs = slice(i * blk, (i + 1) * blk)
                o_ref[0, rs, :] = (oT_ref[i] + out).T.astype(BF16)


def _alibi_slopes():
    return 2.0 ** (-8.0 * np.arange(1, N_HEADS + 1) / N_HEADS)


def _moba_tables(seq):
    nb = seq // MOBA_BLOCK
    slopes = _alibi_slopes()
    key = np.arange(seq)
    feat = np.zeros((seq, HEAD_DIM), np.float32)
    feat[key, key // MOBA_BLOCK] = 1.0
    feat[:, nb] = key % MOBA_BLOCK
    feat[:, nb + 1] = 1.0
    kfeat = np.concatenate([feat] * HEADS_PER_STEP, axis=1)
    qconst = np.zeros((N_HEADS, N_FEAT - nb, MOBA_BLOCK), np.float32)
    qconst[:, 0, :] = slopes[:, None]
    qconst[:, 1, :] = -slopes[:, None] * np.arange(MOBA_BLOCK)[None, :]
    for t in (kfeat, qconst, slopes[:, None] * MOBA_BLOCK * np.arange(-nb, nb)[None, :]):
        t32 = np.asarray(t, np.float32)
        assert np.array_equal(t32.astype(BF16).astype(np.float32), t32)
    return jnp.asarray(kfeat.astype(BF16)), jnp.asarray(qconst, F32)


def _moba(z3, kfeat, qconst):
    B, S, _ = z3.shape
    nb = S // MOBA_BLOCK
    assert nb <= 8 and S % MOBA_BLOCK == 0 and nb + 2 <= N_FEAT
    n_groups = N_HEADS // HEADS_PER_STEP
    q0 = POOL_WIDTH // LANES
    k0 = q0 + ATTN_WIDTH // LANES
    v0 = k0 + ATTN_WIDTH // LANES
    return pl.pallas_call(
        functools.partial(_moba_kernel, seq=S),
        out_shape=jax.ShapeDtypeStruct((B, S, ATTN_WIDTH), BF16),
        grid=(n_groups, B),
        in_specs=[
            pl.BlockSpec((1, S, LANES), lambda g, b: (b, 0, q0 + g)),
            pl.BlockSpec((1, S, LANES), lambda g, b: (b, 0, k0 + g)),
            pl.BlockSpec((1, S, LANES), lambda g, b: (b, 0, v0 + g)),
            _const_spec((S, LANES)),
            pl.BlockSpec((HEADS_PER_STEP, N_FEAT - nb, MOBA_BLOCK), lambda g, b: (g, 0, 0)),
        ],
        out_specs=pl.BlockSpec((1, S, LANES), lambda g, b: (b, 0, g)),
        scratch_shapes=[
            pltpu.VMEM((S, LANES), BF16),
            pltpu.VMEM((LANES, S), F32),
            pltpu.VMEM((nb, LANES, MOBA_BLOCK), BF16),
            pltpu.VMEM((nb, MOBA_BLOCK, MOBA_BLOCK), F32),
            pltpu.VMEM((nb, LANES, MOBA_BLOCK), F32),
        ],
        compiler_params=pltpu.CompilerParams(
            dimension_semantics=("parallel", "parallel"), vmem_limit_bytes=VMEM_LIMIT),
        name="moba",
    )(z3, z3, z3, kfeat, qconst)


def _mix_out_kernel(p_ref, halo_ref, gate_ref, ya_ref, x_ref, pw_ref, ps_ref, wbp_ref,
                    wba_ref, wo_ref, gn_ref, o_ref, ext_ref, *, tm, seq):
    t = pl.program_id(0)
    pos0 = (t * tm) % seq
    pf = p_ref[...].astype(F32)
    halo = halo_ref[...].astype(F32)
    ext_ref[0:HALO, :] = jnp.where(pos0 == 0, 0.0, halo)
    ext_ref[HALO:, :] = pf
    pos = pos0 + lax.broadcasted_iota(jnp.int32, (tm, 1), 0)
    ys = []
    for g, w in enumerate(POOL_WINDOWS):
        cs = slice(g * POOL_GROUP, (g + 1) * POOL_GROUP)
        pg = pf[:, cs]
        win = pg
        for d in range(1, w):
            win = win + ext_ref[HALO - d:HALO - d + tm, cs]
        count = jnp.minimum(pos + 1, w).astype(F32)
        y = win / count - pg
        yg = jnp.dot(y.astype(BF16), pw_ref[g], preferred_element_type=F32)
        ys.append((yg * ps_ref[:, cs]).astype(BF16))
    y_pool = jnp.dot(jnp.concatenate(ys, axis=1), wbp_ref[...], preferred_element_type=F32)
    y_attn = jnp.dot(ya_ref[...], wba_ref[...], preferred_element_type=F32)
    gp = gate_ref[:, :D_MODEL].astype(F32)
    ga = gate_ref[:, D_MODEL:].astype(F32)
    m = y_pool / (1.0 + jnp.exp(-gp)) + y_attn / (1.0 + jnp.exp(-ga))
    mo = jnp.dot(m.astype(BF16), wo_ref[...], preferred_element_type=F32)
    o_ref[...] = x_ref[...] + _rms(mo, gn_ref[...])


def _mix_out(z, y_attn, x2, pw, ps, wbp, wba, wo, gn, *, seq, tm=512):
    T = x2.shape[0]
    assert seq % tm == 0 and tm % HALO == 0
    gate_blk = (POOL_WIDTH + 3 * ATTN_WIDTH) // (2 * D_MODEL)
    return pl.pallas_call(
        functools.partial(_mix_out_kernel, tm=tm, seq=seq),
        out_shape=jax.ShapeDtypeStruct((T, D_MODEL), F32),
        grid=(T // tm,),
        in_specs=[
            pl.BlockSpec((tm, POOL_WIDTH), lambda t: (t, 0)),
            pl.BlockSpec((HALO, POOL_WIDTH), lambda t: (jnp.maximum(t * (tm // HALO) - 1, 0), 0)),
            pl.BlockSpec((tm, 2 * D_MODEL), lambda t: (t, gate_blk)),
            pl.BlockSpec((tm, ATTN_WIDTH), lambda t: (t, 0)),
            pl.BlockSpec((tm, D_MODEL), lambda t: (t, 0)),
            _const_spec(pw.shape),
            _const_spec(ps.shape),
            _const_spec(wbp.shape),
            _const_spec(wba.shape),
            _const_spec(wo.shape),
            _const_spec(gn.shape),
        ],
        out_specs=pl.BlockSpec((tm, D_MODEL), lambda t: (t, 0)),
        scratch_shapes=[pltpu.VMEM((tm + HALO, POOL_WIDTH), F32)],
        compiler_params=pltpu.CompilerParams(
            dimension_semantics=("parallel",), vmem_limit_bytes=VMEM_LIMIT),
        name="mix_out",
    )(z, z, z, y_attn, x2, pw, ps, wbp, wba, wo, gn)


def _mlp_kernel(h_ref, g1_ref, wu_ref, wd_ref, g2_ref, o_ref, *, ff_chunk):
    h = h_ref[...]
    u = _rms(h, g1_ref[...]).astype(BF16)
    acc = jnp.zeros(h.shape, F32)
    for c in range(D_FF // ff_chunk):
        cs = slice(c * ff_chunk, (c + 1) * ff_chunk)
        a = jnp.maximum(jnp.dot(u, wu_ref[:, cs], preferred_element_type=F32), 0.0)
        acc = acc + jnp.dot((a * a).astype(BF16), wd_ref[cs, :], preferred_element_type=F32)
    o_ref[...] = h + _rms(acc, g2_ref[...])


def _mlp(h, g1, wu, wd, g2, *, tm=512, ff_chunk=512):
    T = h.shape[0]
    return pl.pallas_call(
        functools.partial(_mlp_kernel, ff_chunk=ff_chunk),
        out_shape=jax.ShapeDtypeStruct((T, D_MODEL), F32),
        grid=(T // tm,),
        in_specs=[
            pl.BlockSpec((tm, D_MODEL), lambda t: (t, 0)),
            _const_spec(g1.shape),
            _const_spec(wu.shape),
            _const_spec(wd.shape),
            _const_spec(g2.shape),
        ],
        out_specs=pl.BlockSpec((tm, D_MODEL), lambda t: (t, 0)),
        compiler_params=pltpu.CompilerParams(
            dimension_semantics=("parallel",), vmem_limit_bytes=VMEM_LIMIT),
        name="mlp",
    )(h, g1, wu, wd, g2)


def kernel(x, norm_mix_pre, w_in, b_gate, pool_w, pool_scale, w_branch_pool, w_branch_attn, w_out,
           norm_mix_post, norm_mlp_pre, w_up, w_down, norm_mlp_post):
    B, S, D = x.shape
    depth = w_in.shape[0]
    kfeat, qconst = _moba_tables(S)
    h = x.reshape(B * S, D)
    for l in range(depth):
        bias = jnp.concatenate(
            [jnp.zeros((IN_WIDTH - b_gate.shape[1],), F32), b_gate[l]])[None, :]
        z = _in_proj(h, norm_mix_pre[l][None, :], w_in[l].astype(BF16), bias)
        y_attn = _moba(z.reshape(B, S, IN_WIDTH), kfeat, qconst).reshape(B * S, ATTN_WIDTH)
        h = _mix_out(z, y_attn, h, pool_w[l].astype(BF16), pool_scale[l][None, :],
                     w_branch_pool[l].astype(BF16), w_branch_attn[l].astype(BF16),
                     w_out[l].astype(BF16), norm_mix_post[l][None, :], seq=S)
        h = _mlp(h, norm_mlp_pre[l][None, :], w_up[l].astype(BF16), w_down[l].astype(BF16),
                 norm_mlp_post[l][None, :])
    return h.reshape(B, S, D)
```

```python
import functools

import jax
import jax.numpy as jnp
import numpy as np
from jax import lax
from jax.experimental import pallas as pl
from jax.experimental.pallas import tpu as pltpu

D_MODEL = 1024
POOL_WIDTH = 512
POOL_WINDOWS = (2, 4, 8, 16)
POOL_GROUP = 128
N_HEADS = 8
HEAD_DIM = 64
ATTN_WIDTH = N_HEADS * HEAD_DIM
MOBA_BLOCK = 256
MOBA_TOPK = 3
D_FF = 4 * D_MODEL
IN_WIDTH = POOL_WIDTH + 3 * ATTN_WIDTH + 2 * D_MODEL
EPS = 1e-6
NEG = -1e30

LANES = 128
HEADS_PER_STEP = LANES // HEAD_DIM
N_FEAT = LANES - HEAD_DIM
HALO = 16
VMEM_LIMIT = 56 * 1024 * 1024

F32 = jnp.float32
BF16 = jnp.bfloat16


def _rms(x, g):
    return x * lax.rsqrt(jnp.mean(x * x, axis=-1, keepdims=True) + EPS) * g


def _const_spec(shape):
    nd = len(shape)
    return pl.BlockSpec(shape, lambda *_: (0,) * nd, pipeline_mode=pl.Buffered(1))


def _in_proj_kernel(x_ref, g_ref, w_ref, b_ref, z_ref, *, col_chunk):
    u = _rms(x_ref[...], g_ref[...]).astype(BF16)
    for c in range(IN_WIDTH // col_chunk):
        cs = slice(c * col_chunk, (c + 1) * col_chunk)
        z = jnp.dot(u, w_ref[:, cs], preferred_element_type=F32) + b_ref[:, cs]
        z_ref[:, cs] = z.astype(BF16)


def _in_proj(x2, g, w_bf, bias, *, tm=512, col_chunk=512):
    T = x2.shape[0]
    return pl.pallas_call(
        functools.partial(_in_proj_kernel, col_chunk=col_chunk),
        out_shape=jax.ShapeDtypeStruct((T, IN_WIDTH), BF16),
        grid=(T // tm,),
        in_specs=[
            pl.BlockSpec((tm, D_MODEL), lambda t: (t, 0)),
            _const_spec((1, D_MODEL)),
            _const_spec((D_MODEL, IN_WIDTH)),
            _const_spec((1, IN_WIDTH)),
        ],
        out_specs=pl.BlockSpec((tm, IN_WIDTH), lambda t: (t, 0)),
        compiler_params=pltpu.CompilerParams(
            dimension_semantics=("parallel",), vmem_limit_bytes=VMEM_LIMIT),
        name="in_proj",
    )(x2, g, w_bf, bias)


def _moba_kernel(q_ref, k_ref, v_ref, kfeat_ref, qconst_ref, o_ref,
                 kaug_ref, qT_ref, vT_ref, s_ref, oT_ref, *, seq):
    nb = seq // MOBA_BLOCK
    blk = MOBA_BLOCK
    lane = lax.broadcasted_iota(jnp.int32, (1, LANES), 1)
    row = lax.broadcasted_iota(jnp.int32, (LANES, 1), 0)
    r8 = lax.broadcasted_iota(jnp.int32, (nb, blk), 0)
    tri = (lax.broadcasted_iota(jnp.int32, (blk, blk), 0)
           <= lax.broadcasted_iota(jnp.int32, (blk, blk), 1))

    k2 = k_ref[0]
    kmean = k2.astype(F32).reshape(nb, blk, LANES).sum(axis=1) * (1.0 / blk)
    for i in range(nb):
        rs = slice(i * blk, (i + 1) * blk)
        qT_ref[:, rs] = q_ref[0, rs, :].astype(F32).T * (HEAD_DIM ** -0.5)

    for hh in range(HEADS_PER_STEP):
        lo = hh * HEAD_DIM
        head_lanes = (lane >= lo) & (lane < lo + HEAD_DIM)
        head_rows = (row >= lo) & (row < lo + HEAD_DIM)
        kaug_ref[...] = jnp.where(head_lanes, k2, kfeat_ref[...])
        kmean_h = jnp.where(head_lanes, kmean, 0.0)
        for j in range(nb):
            rs = slice(j * blk, (j + 1) * blk)
            vT = v_ref[0, rs, :].astype(F32).T
            vT_ref[j] = jnp.where(head_rows, vT, 0.0).astype(BF16)
        qc = qconst_ref[hh]
        slope = qc[0:1, :]

        for i in range(nb):
            qTi = qT_ref[:, i * blk:(i + 1) * blk]
            if i > MOBA_TOPK:
                gate = jnp.dot(kmean_h, qTi, preferred_element_type=F32,
                               precision=lax.Precision.HIGHEST)
                past = r8 < i
            bias_rows = []
            for j in range(nb):
                if j > i:
                    bias_rows.append(jnp.zeros((1, blk), F32))
                    continue
                bj = slope * float(blk * (j - i))
                if j < i and i > MOBA_TOPK:
                    gj = gate[j:j + 1, :]
                    beats = ((gate > gj) | ((gate == gj) & (r8 < j))) & past
                    cnt = jnp.sum(beats.astype(F32), axis=0, keepdims=True)
                    bj = jnp.where(cnt < float(MOBA_TOPK), bj, NEG)
                bias_rows.append(bj)
            feat = jnp.concatenate(bias_rows + [qc], axis=0)
            qh = qTi[lo:lo + HEAD_DIM]
            parts = [qh, feat] if hh == 0 else [feat, qh]
            qaug = jnp.concatenate(parts, axis=0).astype(BF16)

            mx = None
            for j in range(i + 1):
                s = jnp.dot(kaug_ref[j * blk:(j + 1) * blk, :], qaug,
                            preferred_element_type=F32)
                if j == i:
                    s = jnp.where(tri, s, NEG)
                s_ref[j] = s
                cm = s.reshape(blk // 8, 8, blk).max(axis=0)
                mx = cm if mx is None else jnp.maximum(mx, cm)
            m = mx.max(axis=0, keepdims=True)

            l8 = acc = None
            for j in range(i + 1):
                p = jnp.exp(s_ref[j] - m)
                ps = p.reshape(blk // 8, 8, blk).sum(axis=0)
                pv = jnp.dot(vT_ref[j], p.astype(BF16), preferred_element_type=F32)
                l8 = ps if l8 is None else l8 + ps
                acc = pv if acc is None else acc + pv
            out = acc / l8.sum(axis=0, keepdims=True)
            if hh == 0:
                oT_ref[i] = out
            else:
                rs = slice(i * blk, (i + 1) * blk)
                o_ref[0, rs, :] = (oT_ref[i] + out).T.astype(BF16)


def _alibi_slopes():
    return 2.0 ** (-8.0 * np.arange(1, N_HEADS + 1) / N_HEADS)


def _moba_tables(seq):
    nb = seq // MOBA_BLOCK
    slopes = _alibi_slopes()
    key = np.arange(seq)
    feat = np.zeros((seq, HEAD_DIM), np.float32)
    feat[key, key // MOBA_BLOCK] = 1.0
    feat[:, nb] = key % MOBA_BLOCK
    feat[:, nb + 1] = 1.0
    kfeat = np.concatenate([feat] * HEADS_PER_STEP, axis=1)
    qconst = np.zeros((N_HEADS, N_FEAT - nb, MOBA_BLOCK), np.float32)
    qconst[:, 0, :] = slopes[:, None]
    qconst[:, 1, :] = -slopes[:, None] * np.arange(MOBA_BLOCK)[None, :]
    for t in (kfeat, qconst, slopes[:, None] * MOBA_BLOCK * np.arange(-nb, nb)[None, :]):
        t32 = np.asarray(t, np.float32)
        assert np.array_equal(t32.astype(BF16).astype(np.float32), t32)
    return jnp.asarray(kfeat.astype(BF16)), jnp.asarray(qconst, F32)


def _moba(z3, kfeat, qconst):
    B, S, _ = z3.shape
    nb = S // MOBA_BLOCK
    assert nb <= 8 and S % MOBA_BLOCK == 0 and nb + 2 <= N_FEAT
    n_groups = N_HEADS // HEADS_PER_STEP
    q0 = POOL_WIDTH // LANES
    k0 = q0 + ATTN_WIDTH // LANES
    v0 = k0 + ATTN_WIDTH // LANES
    return pl.pallas_call(
        functools.partial(_moba_kernel, seq=S),
        out_shape=jax.ShapeDtypeStruct((B, S, ATTN_WIDTH), BF16),
        grid=(n_groups, B),
        in_specs=[
            pl.BlockSpec((1, S, LANES), lambda g, b: (b, 0, q0 + g)),
            pl.BlockSpec((1, S, LANES), lambda g, b: (b, 0, k0 + g)),
            pl.BlockSpec((1, S, LANES), lambda g, b: (b, 0, v0 + g)),
            _const_spec((S, LANES)),
            pl.BlockSpec((HEADS_PER_STEP, N_FEAT - nb, MOBA_BLOCK), lambda g, b: (g, 0, 0)),
        ],
        out_specs=pl.BlockSpec((1, S, LANES), lambda g, b: (b, 0, g)),
        scratch_shapes=[
            pltpu.VMEM((S, LANES), BF16),
            pltpu.VMEM((LANES, S), F32),
            pltpu.VMEM((nb, LANES, MOBA_BLOCK), BF16),
            pltpu.VMEM((nb, MOBA_BLOCK, MOBA_BLOCK), F32),
            pltpu.VMEM((nb, LANES, MOBA_BLOCK), F32),
        ],
        compiler_params=pltpu.CompilerParams(
            dimension_semantics=("parallel", "parallel"), vmem_limit_bytes=VMEM_LIMIT),
        name="moba",
    )(z3, z3, z3, kfeat, qconst)


def _mix_out_kernel(p_ref, halo_ref, gate_ref, ya_ref, x_ref, pw_ref, ps_ref, wbp_ref,
                    wba_ref, wo_ref, gn_ref, o_ref, ext_ref, *, tm, seq):
    t = pl.program_id(0)
    pos0 = (t * tm) % seq
    pf = p_ref[...].astype(F32)
    halo = halo_ref[...].astype(F32)
    ext_ref[0:HALO, :] = jnp.where(pos0 == 0, 0.0, halo)
    ext_ref[HALO:, :] = pf
    pos = pos0 + lax.broadcasted_iota(jnp.int32, (tm, 1), 0)
    ys = []
    for g, w in enumerate(POOL_WINDOWS):
        cs = slice(g * POOL_GROUP, (g + 1) * POOL_GROUP)
        pg = pf[:, cs]
        win = pg
        for d in range(1, w):
            win = win + ext_ref[HALO - d:HALO - d + tm, cs]
        count = jnp.minimum(pos + 1, w).astype(F32)
        y = win / count - pg
        yg = jnp.dot(y.astype(BF16), pw_ref[g], preferred_element_type=F32)
        ys.append((yg * ps_ref[:, cs]).astype(BF16))
    y_pool = jnp.dot(jnp.concatenate(ys, axis=1), wbp_ref[...], preferred_element_type=F32)
    y_attn = jnp.dot(ya_ref[...], wba_ref[...], preferred_element_type=F32)
    gp = gate_ref[:, :D_MODEL].astype(F32)
    ga = gate_ref[:, D_MODEL:].astype(F32)
    m = y_pool / (1.0 + jnp.exp(-gp)) + y_attn / (1.0 + jnp.exp(-ga))
    mo = jnp.dot(m.astype(BF16), wo_ref[...], preferred_element_type=F32)
    o_ref[...] = x_ref[...] + _rms(mo, gn_ref[...])


def _mix_out(z, y_attn, x2, pw, ps, wbp, wba, wo, gn, *, seq, tm=512):
    T = x2.shape[0]
    assert seq % tm == 0 and tm % HALO == 0
    gate_blk = (POOL_WIDTH + 3 * ATTN_WIDTH) // (2 * D_MODEL)
    return pl.pallas_call(
        functools.partial(_mix_out_kernel, tm=tm, seq=seq),
        out_shape=jax.ShapeDtypeStruct((T, D_MODEL), F32),
        grid=(T // tm,),
        in_specs=[
            pl.BlockSpec((tm, POOL_WIDTH), lambda t: (t, 0)),
            pl.BlockSpec((HALO, POOL_WIDTH), lambda t: (jnp.maximum(t * (tm // HALO) - 1, 0), 0)),
            pl.BlockSpec((tm, 2 * D_MODEL), lambda t: (t, gate_blk)),
            pl.BlockSpec((tm, ATTN_WIDTH), lambda t: (t, 0)),
            pl.BlockSpec((tm, D_MODEL), lambda t: (t, 0)),
            _const_spec(pw.shape),
            _const_spec(ps.shape),
            _const_spec(wbp.shape),
            _const_spec(wba.shape),
            _const_spec(wo.shape),
            _const_spec(gn.shape),
        ],
        out_specs=pl.BlockSpec((tm, D_MODEL), lambda t: (t, 0)),
        scratch_shapes=[pltpu.VMEM((tm + HALO, POOL_WIDTH), F32)],
        compiler_params=pltpu.CompilerParams(
            dimension_semantics=("parallel",), vmem_limit_bytes=VMEM_LIMIT),
        name="mix_out",
    )(z, z, z, y_attn, x2, pw, ps, wbp, wba, wo, gn)


def _mlp_kernel(h_ref, g1_ref, wu_ref, wd_ref, g2_ref, o_ref, *, ff_chunk):
    h = h_ref[...]
    u = _rms(h, g1_ref[...]).astype(BF16)
    acc = jnp.zeros(h.shape, F32)
    for c in range(D_FF // ff_chunk):
        cs = slice(c * ff_chunk, (c + 1) * ff_chunk)
        a = jnp.maximum(jnp.dot(u, wu_ref[:, cs], preferred_element_type=F32), 0.0)
        acc = acc + jnp.dot((a * a).astype(BF16), wd_ref[cs, :], preferred_element_type=F32)
    o_ref[...] = h + _rms(acc, g2_ref[...])


def _mlp(h, g1, wu, wd, g2, *, tm=512, ff_chunk=512):
    T = h.shape[0]
    return pl.pallas_call(
        functools.partial(_mlp_kernel, ff_chunk=ff_chunk),
        out_shape=jax.ShapeDtypeStruct((T, D_MODEL), F32),
        grid=(T // tm,),
        in_specs=[
            pl.BlockSpec((tm, D_MODEL), lambda t: (t, 0)),
            _const_spec(g1.shape),
            _const_spec(wu.shape),
            _const_spec(wd.shape),
            _const_spec(g2.shape),
        ],
        out_specs=pl.BlockSpec((tm, D_MODEL), lambda t: (t, 0)),
        compiler_params=pltpu.CompilerParams(
            dimension_semantics=("parallel",), vmem_limit_bytes=VMEM_LIMIT),
        name="mlp",
    )(h, g1, wu, wd, g2)


def kernel(x, norm_mix_pre, w_in, b_gate, pool_w, pool_scale, w_branch_pool, w_branch_attn, w_out,
           norm_mix_post, norm_mlp_pre, w_up, w_down, norm_mlp_post):
    B, S, D = x.shape
    depth = w_in.shape[0]
    kfeat, qconst = _moba_tables(S)
    h = x.reshape(B * S, D)
    for l in range(depth):
        bias = jnp.concatenate(
            [jnp.zeros((IN_WIDTH - b_gate.shape[1],), F32), b_gate[l]])[None, :]
        z = _in_proj(h, norm_mix_pre[l][None, :], w_in[l].astype(BF16), bias)
        y_attn = _moba(z.reshape(B, S, IN_WIDTH), kfeat, qconst).reshape(B * S, ATTN_WIDTH)
        h = _mix_out(z, y_attn, h, pool_w[l].astype(BF16), pool_scale[l][None, :],
                     w_branch_pool[l].astype(BF16), w_branch_attn[l].astype(BF16),
                     w_out[l].astype(BF16), norm_mix_post[l][None, :], seq=S)
        h = _mlp(h, norm_mlp_pre[l][None, :], w_up[l].astype(BF16), w_down[l].astype(BF16),
                 norm_mlp_post[l][None, :])
    return h.reshape(B, S, D)
```

```python
import functools

import jax
import jax.numpy as jnp
import numpy as np
from jax import lax
from jax.experimental import pallas as pl
from jax.experimental.pallas import tpu as pltpu

D_MODEL = 1024
POOL_WIDTH = 512
POOL_WINDOWS = (2, 4, 8, 16)
POOL_GROUP = 128
N_HEADS = 8
HEAD_DIM = 64
ATTN_WIDTH = N_HEADS * HEAD_DIM
MOBA_BLOCK = 256
MOBA_TOPK = 3
D_FF = 4 * D_MODEL
IN_WIDTH = POOL_WIDTH + 3 * ATTN_WIDTH + 2 * D_MODEL
EPS = 1e-6
NEG = -1e30

LANES = 128
HEADS_PER_STEP = LANES // HEAD_DIM
N_FEAT = LANES - HEAD_DIM
PV_ROWS = HEAD_DIM + 16
PIPE_DEPTH = 3
PV_CHAINS = 2
HALO = 16
VMEM_LIMIT = 56 * 1024 * 1024

F32 = jnp.float32
BF16 = jnp.bfloat16


def _rms(x, g):
    return x * lax.rsqrt(jnp.mean(x * x, axis=-1, keepdims=True) + EPS) * g


def _const_spec(shape):
    nd = len(shape)
    return pl.BlockSpec(shape, lambda *_: (0,) * nd, pipeline_mode=pl.Buffered(1))


def _in_proj_kernel(x_ref, g_ref, w_ref, b_ref, z_ref, *, col_chunk):
    u = _rms(x_ref[...], g_ref[...]).astype(BF16)
    for c in range(IN_WIDTH // col_chunk):
        cs = slice(c * col_chunk, (c + 1) * col_chunk)
        z = jnp.dot(u, w_ref[:, cs], preferred_element_type=F32) + b_ref[:, cs]
        z_ref[:, cs] = z.astype(BF16)


def _in_proj(x2, g, w_bf, bias, *, tm=512, col_chunk=512):
    T = x2.shape[0]
    return pl.pallas_call(
        functools.partial(_in_proj_kernel, col_chunk=col_chunk),
        out_shape=jax.ShapeDtypeStruct((T, IN_WIDTH), BF16),
        grid=(T // tm,),
        in_specs=[
            pl.BlockSpec((tm, D_MODEL), lambda t: (t, 0)),
            _const_spec((1, D_MODEL)),
            _const_spec((D_MODEL, IN_WIDTH)),
            _const_spec((1, IN_WIDTH)),
        ],
        out_specs=pl.BlockSpec((tm, IN_WIDTH), lambda t: (t, 0)),
        compiler_params=pltpu.CompilerParams(
            dimension_semantics=("parallel",), vmem_limit_bytes=VMEM_LIMIT),
        name="in_proj",
    )(x2, g, w_bf, bias)


def _moba_kernel(q_ref, k_ref, v_ref, kfeat_ref, qconst_ref, o_ref,
                 kaug_ref, qT_ref, vT_ref, s_ref, oT_ref, *, seq):
    nb = seq // MOBA_BLOCK
    blk = MOBA_BLOCK
    lane = lax.broadcasted_iota(jnp.int32, (1, LANES), 1)
    r8 = lax.broadcasted_iota(jnp.int32, (nb, blk), 0)
    tri = (lax.broadcasted_iota(jnp.int32, (blk, blk), 0)
           <= lax.broadcasted_iota(jnp.int32, (blk, blk), 1))
    ones_rows = (lax.broadcasted_iota(jnp.int32, (PV_ROWS - HEAD_DIM, blk), 0) == 0).astype(F32)

    k2 = k_ref[0]
    kmean = k2.astype(F32).reshape(nb, blk, LANES).sum(axis=1) * (1.0 / blk)
    kmean_h = []
    for hh in range(HEADS_PER_STEP):
        lo = hh * HEAD_DIM
        head_lanes = (lane >= lo) & (lane < lo + HEAD_DIM)
        kaug_ref[hh] = jnp.where(head_lanes, k2, kfeat_ref[...])
        kmean_h.append(jnp.where(head_lanes, kmean, 0.0))
    for j in range(nb):
        rs = slice(j * blk, (j + 1) * blk)
        qT_ref[:, rs] = q_ref[0, rs, :].astype(F32).T * (HEAD_DIM ** -0.5)
        vT = v_ref[0, rs, :].astype(F32).T
        for hh in range(HEADS_PER_STEP):
            vh = vT[hh * HEAD_DIM:(hh + 1) * HEAD_DIM]
            vT_ref[hh, j] = jnp.concatenate([vh, ones_rows], axis=0).astype(BF16)

    def query_operand(i, hh):
        lo = hh * HEAD_DIM
        qc = qconst_ref[hh]
        slope = qc[0:1, :]
        qTi = qT_ref[:, i * blk:(i + 1) * blk]
        if i > MOBA_TOPK:
            gate = jnp.dot(kmean_h[hh], qTi, preferred_element_type=F32,
                           precision=lax.Precision.HIGHEST)
            past = r8 < i
        bias_rows = []
        for j in range(nb):
            if j > i:
                bias_rows.append(jnp.zeros((1, blk), F32))
                continue
            bj = slope * float(blk * (j - i))
            if j < i and i > MOBA_TOPK:
                gj = gate[j:j + 1, :]
                beats = ((gate > gj) | ((gate == gj) & (r8 < j))) & past
                cnt = jnp.sum(beats.astype(F32), axis=0, keepdims=True)
                bj = jnp.where(cnt < float(MOBA_TOPK), bj, NEG)
            bias_rows.append(bj)
        feat = jnp.concatenate(bias_rows + [qc], axis=0)
        qh = qTi[lo:lo + HEAD_DIM]
        parts = [qh, feat] if hh == 0 else [feat, qh]
        return jnp.concatenate(parts, axis=0).astype(BF16)

    def score_block(slot, i, hh, j, qaug, mx):
        s = jnp.dot(kaug_ref[hh, j * blk:(j + 1) * blk, :], qaug,
                    preferred_element_type=F32)
        if j == i:
            s = jnp.where(tri, s, NEG)
        s_ref[slot, j] = s
        cm = s.reshape(blk // 8, 8, blk).max(axis=0)
        return cm if mx is None else jnp.maximum(mx, cm)

    def value_block(slot, hh, j, m, acc):
        p = jnp.exp(s_ref[slot, j] - m).astype(BF16)
        pv = jnp.dot(vT_ref[hh, j], p, preferred_element_type=F32)
        return pv if acc is None else acc + pv

    def emit(i, hh, acc):
        out = acc[:HEAD_DIM] / acc[HEAD_DIM:HEAD_DIM + 1]
        oT_ref[i, hh * HEAD_DIM:(hh + 1) * HEAD_DIM, :] = out
        if hh == HEADS_PER_STEP - 1:
            o_ref[0, i * blk:(i + 1) * blk, :] = oT_ref[i].T.astype(BF16)

    def pass1(slot, i, hh):
        qaug = query_operand(i, hh)
        mx = None
        for j in range(i + 1):
            mx = score_block(slot, i, hh, j, qaug, mx)
        return mx.max(axis=0, keepdims=True)

    def pass2(slot, i, hh, m):
        accs = [None] * PV_CHAINS
        for j in range(i + 1):
            accs[j % PV_CHAINS] = value_block(slot, hh, j, m, accs[j % PV_CHAINS])
        acc = accs[0]
        for a in accs[1:]:
            if a is not None:
                acc = acc + a
        emit(i, hh, acc)

    segs = [(i, hh) for i in range(nb) for hh in range(HEADS_PER_STEP)]
    ms = {}
    for t in range(min(PIPE_DEPTH - 1, len(segs))):
        ms[t] = pass1(t % PIPE_DEPTH, *segs[t])
    for t, (i, hh) in enumerate(segs):
        ahead = t + PIPE_DEPTH - 1
        if ahead < len(segs):
            ms[ahead] = pass1(ahead % PIPE_DEPTH, *segs[ahead])
        pass2(t % PIPE_DEPTH, i, hh, ms.pop(t))


def _alibi_slopes():
    return 2.0 ** (-8.0 * np.arange(1, N_HEADS + 1) / N_HEADS)


def _moba_tables(seq):
    nb = seq // MOBA_BLOCK
    slopes = _alibi_slopes()
    key = np.arange(seq)
    feat = np.zeros((seq, HEAD_DIM), np.float32)
    feat[key, key // MOBA_BLOCK] = 1.0
    feat[:, nb] = key % MOBA_BLOCK
    feat[:, nb + 1] = 1.0
    kfeat = np.concatenate([feat] * HEADS_PER_STEP, axis=1)
    qconst = np.zeros((N_HEADS, N_FEAT - nb, MOBA_BLOCK), np.float32)
    qconst[:, 0, :] = slopes[:, None]
    qconst[:, 1, :] = -slopes[:, None] * np.arange(MOBA_BLOCK)[None, :]
    for t in (kfeat, qconst, slopes[:, None] * MOBA_BLOCK * np.arange(-nb, nb)[None, :]):
        t32 = np.asarray(t, np.float32)
        assert np.array_equal(t32.astype(BF16).astype(np.float32), t32)
    return jnp.asarray(kfeat.astype(BF16)), jnp.asarray(qconst, F32)


def _moba(z3, kfeat, qconst):
    B, S, _ = z3.shape
    nb = S // MOBA_BLOCK
    assert nb <= 8 and S % MOBA_BLOCK == 0 and nb + 2 <= N_FEAT
    n_groups = N_HEADS // HEADS_PER_STEP
    q0 = POOL_WIDTH // LANES
    k0 = q0 + ATTN_WIDTH // LANES
    v0 = k0 + ATTN_WIDTH // LANES
    return pl.pallas_call(
        functools.partial(_moba_kernel, seq=S),
        out_shape=jax.ShapeDtypeStruct((B, S, ATTN_WIDTH), BF16),
        grid=(n_groups, B),
        in_specs=[
            pl.BlockSpec((1, S, LANES), lambda g, b: (b, 0, q0 + g)),
            pl.BlockSpec((1, S, LANES), lambda g, b: (b, 0, k0 + g)),
            pl.BlockSpec((1, S, LANES), lambda g, b: (b, 0, v0 + g)),
            _const_spec((S, LANES)),
            pl.BlockSpec((HEADS_PER_STEP, N_FEAT - nb, MOBA_BLOCK), lambda g, b: (g, 0, 0)),
        ],
        out_specs=pl.BlockSpec((1, S, LANES), lambda g, b: (b, 0, g)),
        scratch_shapes=[
            pltpu.VMEM((HEADS_PER_STEP, S, LANES), BF16),
            pltpu.VMEM((LANES, S), F32),
            pltpu.VMEM((HEADS_PER_STEP, nb, PV_ROWS, MOBA_BLOCK), BF16),
            pltpu.VMEM((PIPE_DEPTH, nb, MOBA_BLOCK, MOBA_BLOCK), F32),
            pltpu.VMEM((nb, LANES, MOBA_BLOCK), F32),
        ],
        compiler_params=pltpu.CompilerParams(
            dimension_semantics=("parallel", "parallel"), vmem_limit_bytes=VMEM_LIMIT),
        name="moba",
    )(z3, z3, z3, kfeat, qconst)


def _mix_out_kernel(p_ref, halo_ref, gate_ref, ya_ref, x_ref, pw_ref, ps_ref, wbp_ref,
                    wba_ref, wo_ref, gn_ref, o_ref, ext_ref, *, tm, seq):
    t = pl.program_id(0)
    pos0 = (t * tm) % seq
    pf = p_ref[...].astype(F32)
    halo = halo_ref[...].astype(F32)
    ext_ref[0:HALO, :] = jnp.where(pos0 == 0, 0.0, halo)
    ext_ref[HALO:, :] = pf
    pos = pos0 + lax.broadcasted_iota(jnp.int32, (tm, 1), 0)
    ys = []
    for g, w in enumerate(POOL_WINDOWS):
        cs = slice(g * POOL_GROUP, (g + 1) * POOL_GROUP)
        pg = pf[:, cs]
        win = pg
        for d in range(1, w):
            win = win + ext_ref[HALO - d:HALO - d + tm, cs]
        count = jnp.minimum(pos + 1, w).astype(F32)
        y = win / count - pg
        yg = jnp.dot(y.astype(BF16), pw_ref[g], preferred_element_type=F32)
        ys.append((yg * ps_ref[:, cs]).astype(BF16))
    y_pool = jnp.dot(jnp.concatenate(ys, axis=1), wbp_ref[...], preferred_element_type=F32)
    y_attn = jnp.dot(ya_ref[...], wba_ref[...], preferred_element_type=F32)
    gp = gate_ref[:, :D_MODEL].astype(F32)
    ga = gate_ref[:, D_MODEL:].astype(F32)
    m = y_pool / (1.0 + jnp.exp(-gp)) + y_attn / (1.0 + jnp.exp(-ga))
    mo = jnp.dot(m.astype(BF16), wo_ref[...], preferred_element_type=F32)
    o_ref[...] = x_ref[...] + _rms(mo, gn_ref[...])


def _mix_out(z, y_attn, x2, pw, ps, wbp, wba, wo, gn, *, seq, tm=512):
    T = x2.shape[0]
    assert seq % tm == 0 and tm % HALO == 0
    gate_blk = (POOL_WIDTH + 3 * ATTN_WIDTH) // (2 * D_MODEL)
    return pl.pallas_call(
        functools.partial(_mix_out_kernel, tm=tm, seq=seq),
        out_shape=jax.ShapeDtypeStruct((T, D_MODEL), F32),
        grid=(T // tm,),
        in_specs=[
            pl.BlockSpec((tm, POOL_WIDTH), lambda t: (t, 0)),
            pl.BlockSpec((HALO, POOL_WIDTH), lambda t: (jnp.maximum(t * (tm // HALO) - 1, 0), 0)),
            pl.BlockSpec((tm, 2 * D_MODEL), lambda t: (t, gate_blk)),
            pl.BlockSpec((tm, ATTN_WIDTH), lambda t: (t, 0)),
            pl.BlockSpec((tm, D_MODEL), lambda t: (t, 0)),
            _const_spec(pw.shape),
            _const_spec(ps.shape),
            _const_spec(wbp.shape),
            _const_spec(wba.shape),
            _const_spec(wo.shape),
            _const_spec(gn.shape),
        ],
        out_specs=pl.BlockSpec((tm, D_MODEL), lambda t: (t, 0)),
        scratch_shapes=[pltpu.VMEM((tm + HALO, POOL_WIDTH), F32)],
        compiler_params=pltpu.CompilerParams(
            dimension_semantics=("parallel",), vmem_limit_bytes=VMEM_LIMIT),
        name="mix_out",
    )(z, z, z, y_attn, x2, pw, ps, wbp, wba, wo, gn)


def _mlp_kernel(h_ref, g1_ref, wu_ref, wd_ref, g2_ref, o_ref, *, ff_chunk):
    h = h_ref[...]
    u = _rms(h, g1_ref[...]).astype(BF16)
    acc = jnp.zeros(h.shape, F32)
    for c in range(D_FF // ff_chunk):
        cs = slice(c * ff_chunk, (c + 1) * ff_chunk)
        a = jnp.maximum(jnp.dot(u, wu_ref[:, cs], preferred_element_type=F32), 0.0)
        acc = acc + jnp.dot((a * a).astype(BF16), wd_ref[cs, :], preferred_element_type=F32)
    o_ref[...] = h + _rms(acc, g2_ref[...])


def _mlp(h, g1, wu, wd, g2, *, tm=512, ff_chunk=512):
    T = h.shape[0]
    return pl.pallas_call(
        functools.partial(_mlp_kernel, ff_chunk=ff_chunk),
        out_shape=jax.ShapeDtypeStruct((T, D_MODEL), F32),
        grid=(T // tm,),
        in_specs=[
            pl.BlockSpec((tm, D_MODEL), lambda t: (t, 0)),
            _const_spec(g1.shape),
            _const_spec(wu.shape),
            _const_spec(wd.shape),
            _const_spec(g2.shape),
        ],
        out_specs=pl.BlockSpec((tm, D_MODEL), lambda t: (t, 0)),
        compiler_params=pltpu.CompilerParams(
            dimension_semantics=("parallel",), vmem_limit_bytes=VMEM_LIMIT),
        name="mlp",
    )(h, g1, wu, wd, g2)


def kernel(x, norm_mix_pre, w_in, b_gate, pool_w, pool_scale, w_branch_pool, w_branch_attn, w_out,
           norm_mix_post, norm_mlp_pre, w_up, w_down, norm_mlp_post):
    B, S, D = x.shape
    depth = w_in.shape[0]
    kfeat, qconst = _moba_tables(S)
    h = x.reshape(B * S, D)
    for l in range(depth):
        bias = jnp.concatenate(
            [jnp.zeros((IN_WIDTH - b_gate.shape[1],), F32), b_gate[l]])[None, :]
        z = _in_proj(h, norm_mix_pre[l][None, :], w_in[l].astype(BF16), bias)
        y_attn = _moba(z.reshape(B, S, IN_WIDTH), kfeat, qconst).reshape(B * S, ATTN_WIDTH)
        h = _mix_out(z, y_attn, h, pool_w[l].astype(BF16), pool_scale[l][None, :],
                     w_branch_pool[l].astype(BF16), w_branch_attn[l].astype(BF16),
                     w_out[l].astype(BF16), norm_mix_post[l][None, :], seq=S)
        h = _mlp(h, norm_mlp_pre[l][None, :], w_up[l].astype(BF16), w_down[l].astype(BF16),
                 norm_mlp_post[l][None, :])
    return h.reshape(B, S, D)
```

```python
import functools

import jax
import jax.numpy as jnp
import numpy as np
from jax import lax
from jax.experimental import pallas as pl
from jax.experimental.pallas import tpu as pltpu

D_MODEL = 1024
POOL_WIDTH = 512
POOL_WINDOWS = (2, 4, 8, 16)
POOL_GROUP = 128
N_HEADS = 8
HEAD_DIM = 64
ATTN_WIDTH = N_HEADS * HEAD_DIM
MOBA_BLOCK = 256
MOBA_TOPK = 3
D_FF = 4 * D_MODEL
IN_WIDTH = POOL_WIDTH + 3 * ATTN_WIDTH + 2 * D_MODEL
EPS = 1e-6
NEG = -1e30

LANES = 128
HEADS_PER_STEP = LANES // HEAD_DIM
N_FEAT = LANES - HEAD_DIM
PV_ROWS = HEAD_DIM + 16
PIPE_DEPTH = 3
PV_CHAINS = 2
HALO = 16
VMEM_LIMIT = 56 * 1024 * 1024

F32 = jnp.float32
BF16 = jnp.bfloat16


def _rms(x, g):
    return x * lax.rsqrt(jnp.mean(x * x, axis=-1, keepdims=True) + EPS) * g


def _const_spec(shape):
    nd = len(shape)
    return pl.BlockSpec(shape, lambda *_: (0,) * nd, pipeline_mode=pl.Buffered(1))


def _in_proj_kernel(x_ref, g_ref, w_ref, b_ref, z_ref, *, col_chunk):
    u = _rms(x_ref[...], g_ref[...]).astype(BF16)
    for c in range(IN_WIDTH // col_chunk):
        cs = slice(c * col_chunk, (c + 1) * col_chunk)
        z = jnp.dot(u, w_ref[:, cs], preferred_element_type=F32) + b_ref[:, cs]
        z_ref[:, cs] = z.astype(BF16)


def _in_proj(x2, g, w_bf, bias, *, tm=1024, col_chunk=512):
    T = x2.shape[0]
    return pl.pallas_call(
        functools.partial(_in_proj_kernel, col_chunk=col_chunk),
        out_shape=jax.ShapeDtypeStruct((T, IN_WIDTH), BF16),
        grid=(T // tm,),
        in_specs=[
            pl.BlockSpec((tm, D_MODEL), lambda t: (t, 0)),
            _const_spec((1, D_MODEL)),
            _const_spec((D_MODEL, IN_WIDTH)),
            _const_spec((1, IN_WIDTH)),
        ],
        out_specs=pl.BlockSpec((tm, IN_WIDTH), lambda t: (t, 0)),
        compiler_params=pltpu.CompilerParams(
            dimension_semantics=("parallel",), vmem_limit_bytes=VMEM_LIMIT),
        name="in_proj",
    )(x2, g, w_bf, bias)


def _moba_kernel(q_ref, k_ref, v_ref, kfeat_ref, qconst_ref, o_ref,
                 kaug_ref, qT_ref, vT_ref, s_ref, oT_ref, *, seq):
    nb = seq // MOBA_BLOCK
    blk = MOBA_BLOCK
    lane = lax.broadcasted_iota(jnp.int32, (1, LANES), 1)
    r8 = lax.broadcasted_iota(jnp.int32, (nb, blk), 0)
    tri = (lax.broadcasted_iota(jnp.int32, (blk, blk), 0)
           <= lax.broadcasted_iota(jnp.int32, (blk, blk), 1))
    ones_rows = (lax.broadcasted_iota(jnp.int32, (PV_ROWS - HEAD_DIM, blk), 0) == 0).astype(F32)

    k2 = k_ref[0]
    kmean = k2.astype(F32).reshape(nb, blk, LANES).sum(axis=1) * (1.0 / blk)
    kmean_h = []
    for hh in range(HEADS_PER_STEP):
        lo = hh * HEAD_DIM
        head_lanes = (lane >= lo) & (lane < lo + HEAD_DIM)
        kaug_ref[hh] = jnp.where(head_lanes, k2, kfeat_ref[...])
        kmean_h.append(jnp.where(head_lanes, kmean, 0.0))
    for j in range(nb):
        rs = slice(j * blk, (j + 1) * blk)
        qT_ref[:, rs] = q_ref[0, rs, :].astype(F32).T * (HEAD_DIM ** -0.5)
        vT = v_ref[0, rs, :].astype(F32).T
        for hh in range(HEADS_PER_STEP):
            vh = vT[hh * HEAD_DIM:(hh + 1) * HEAD_DIM]
            vT_ref[hh, j] = jnp.concatenate([vh, ones_rows], axis=0).astype(BF16)

    def query_operand(i, hh):
        lo = hh * HEAD_DIM
        qc = qconst_ref[hh]
        slope = qc[0:1, :]
        qTi = qT_ref[:, i * blk:(i + 1) * blk]
        if i > MOBA_TOPK:
            gate = jnp.dot(kmean_h[hh], qTi, preferred_element_type=F32,
                           precision=lax.Precision.HIGHEST)
            past = r8 < i
        bias_rows = []
        for j in range(nb):
            if j > i:
                bias_rows.append(jnp.zeros((1, blk), F32))
                continue
            bj = slope * float(blk * (j - i))
            if j < i and i > MOBA_TOPK:
                gj = gate[j:j + 1, :]
                beats = ((gate > gj) | ((gate == gj) & (r8 < j))) & past
                cnt = jnp.sum(beats.astype(F32), axis=0, keepdims=True)
                bj = jnp.where(cnt < float(MOBA_TOPK), bj, NEG)
            bias_rows.append(bj)
        feat = jnp.concatenate(bias_rows + [qc], axis=0)
        qh = qTi[lo:lo + HEAD_DIM]
        parts = [qh, feat] if hh == 0 else [feat, qh]
        return jnp.concatenate(parts, axis=0).astype(BF16)

    def score_block(slot, i, hh, j, qaug, mx):
        s = jnp.dot(kaug_ref[hh, j * blk:(j + 1) * blk, :], qaug,
                    preferred_element_type=F32)
        if j == i:
            s = jnp.where(tri, s, NEG)
        s_ref[slot, j] = s
        cm = s.reshape(blk // 8, 8, blk).max(axis=0)
        return cm if mx is None else jnp.maximum(mx, cm)

    def value_block(slot, hh, j, m, acc):
        p = jnp.exp(s_ref[slot, j] - m).astype(BF16)
        pv = jnp.dot(vT_ref[hh, j], p, preferred_element_type=F32)
        return pv if acc is None else acc + pv

    def emit(i, hh, acc):
        out = acc[:HEAD_DIM] / acc[HEAD_DIM:HEAD_DIM + 1]
        oT_ref[i, hh * HEAD_DIM:(hh + 1) * HEAD_DIM, :] = out
        if hh == HEADS_PER_STEP - 1:
            o_ref[0, i * blk:(i + 1) * blk, :] = oT_ref[i].T.astype(BF16)

    def pass1(slot, i, hh):
        qaug = query_operand(i, hh)
        mx = None
        for j in range(i + 1):
            mx = score_block(slot, i, hh, j, qaug, mx)
        return mx.max(axis=0, keepdims=True)

    def pass2(slot, i, hh, m):
        accs = [None] * PV_CHAINS
        for j in range(i + 1):
            accs[j % PV_CHAINS] = value_block(slot, hh, j, m, accs[j % PV_CHAINS])
        acc = accs[0]
        for a in accs[1:]:
            if a is not None:
                acc = acc + a
        emit(i, hh, acc)

    segs = [(i, hh) for i in range(nb) for hh in range(HEADS_PER_STEP)]
    ms = {}
    for t in range(min(PIPE_DEPTH - 1, len(segs))):
        ms[t] = pass1(t % PIPE_DEPTH, *segs[t])
    for t, (i, hh) in enumerate(segs):
        ahead = t + PIPE_DEPTH - 1
        if ahead < len(segs):
            ms[ahead] = pass1(ahead % PIPE_DEPTH, *segs[ahead])
        pass2(t % PIPE_DEPTH, i, hh, ms.pop(t))


def _alibi_slopes():
    return 2.0 ** (-8.0 * np.arange(1, N_HEADS + 1) / N_HEADS)


def _moba_tables(seq):
    nb = seq // MOBA_BLOCK
    slopes = _alibi_slopes()
    key = np.arange(seq)
    feat = np.zeros((seq, HEAD_DIM), np.float32)
    feat[key, key // MOBA_BLOCK] = 1.0
    feat[:, nb] = key % MOBA_BLOCK
    feat[:, nb + 1] = 1.0
    kfeat = np.concatenate([feat] * HEADS_PER_STEP, axis=1)
    qconst = np.zeros((N_HEADS, N_FEAT - nb, MOBA_BLOCK), np.float32)
    qconst[:, 0, :] = slopes[:, None]
    qconst[:, 1, :] = -slopes[:, None] * np.arange(MOBA_BLOCK)[None, :]
    for t in (kfeat, qconst, slopes[:, None] * MOBA_BLOCK * np.arange(-nb, nb)[None, :]):
        t32 = np.asarray(t, np.float32)
        assert np.array_equal(t32.astype(BF16).astype(np.float32), t32)
    return jnp.asarray(kfeat.astype(BF16)), jnp.asarray(qconst, F32)


def _moba(z3, kfeat, qconst):
    B, S, _ = z3.shape
    nb = S // MOBA_BLOCK
    assert nb <= 8 and S % MOBA_BLOCK == 0 and nb + 2 <= N_FEAT
    n_groups = N_HEADS // HEADS_PER_STEP
    q0 = POOL_WIDTH // LANES
    k0 = q0 + ATTN_WIDTH // LANES
    v0 = k0 + ATTN_WIDTH // LANES
    return pl.pallas_call(
        functools.partial(_moba_kernel, seq=S),
        out_shape=jax.ShapeDtypeStruct((B, S, ATTN_WIDTH), BF16),
        grid=(n_groups, B),
        in_specs=[
            pl.BlockSpec((1, S, LANES), lambda g, b: (b, 0, q0 + g)),
            pl.BlockSpec((1, S, LANES), lambda g, b: (b, 0, k0 + g)),
            pl.BlockSpec((1, S, LANES), lambda g, b: (b, 0, v0 + g)),
            _const_spec((S, LANES)),
            pl.BlockSpec((HEADS_PER_STEP, N_FEAT - nb, MOBA_BLOCK), lambda g, b: (g, 0, 0)),
        ],
        out_specs=pl.BlockSpec((1, S, LANES), lambda g, b: (b, 0, g)),
        scratch_shapes=[
            pltpu.VMEM((HEADS_PER_STEP, S, LANES), BF16),
            pltpu.VMEM((LANES, S), F32),
            pltpu.VMEM((HEADS_PER_STEP, nb, PV_ROWS, MOBA_BLOCK), BF16),
            pltpu.VMEM((PIPE_DEPTH, nb, MOBA_BLOCK, MOBA_BLOCK), F32),
            pltpu.VMEM((nb, LANES, MOBA_BLOCK), F32),
        ],
        compiler_params=pltpu.CompilerParams(
            dimension_semantics=("parallel", "parallel"), vmem_limit_bytes=VMEM_LIMIT),
        name="moba",
    )(z3, z3, z3, kfeat, qconst)


def _mix_out_kernel(p_ref, halo_ref, gate_ref, ya_ref, x_ref, pw_ref, ps_ref, wbp_ref,
                    wba_ref, wo_ref, gn_ref, o_ref, ext_ref, *, tm, seq):
    t = pl.program_id(0)
    pos0 = (t * tm) % seq
    pf = p_ref[...].astype(F32)
    halo = halo_ref[...].astype(F32)
    ext_ref[0:HALO, :] = jnp.where(pos0 == 0, 0.0, halo)
    ext_ref[HALO:, :] = pf
    pos = pos0 + lax.broadcasted_iota(jnp.int32, (tm, 1), 0)
    ys = []
    for g, w in enumerate(POOL_WINDOWS):
        cs = slice(g * POOL_GROUP, (g + 1) * POOL_GROUP)
        pg = pf[:, cs]
        win = pg
        for d in range(1, w):
            win = win + ext_ref[HALO - d:HALO - d + tm, cs]
        count = jnp.minimum(pos + 1, w).astype(F32)
        y = win / count - pg
        yg = jnp.dot(y.astype(BF16), pw_ref[g], preferred_element_type=F32)
        ys.append((yg * ps_ref[:, cs]).astype(BF16))
    y_pool = jnp.dot(jnp.concatenate(ys, axis=1), wbp_ref[...], preferred_element_type=F32)
    y_attn = jnp.dot(ya_ref[...], wba_ref[...], preferred_element_type=F32)
    gp = gate_ref[:, :D_MODEL].astype(F32)
    ga = gate_ref[:, D_MODEL:].astype(F32)
    m = y_pool / (1.0 + jnp.exp(-gp)) + y_attn / (1.0 + jnp.exp(-ga))
    mo = jnp.dot(m.astype(BF16), wo_ref[...], preferred_element_type=F32)
    o_ref[...] = x_ref[...] + _rms(mo, gn_ref[...])


def _mix_out(z, y_attn, x2, pw, ps, wbp, wba, wo, gn, *, seq, tm=1024):
    T = x2.shape[0]
    assert seq % tm == 0 and tm % HALO == 0
    gate_blk = (POOL_WIDTH + 3 * ATTN_WIDTH) // (2 * D_MODEL)
    return pl.pallas_call(
        functools.partial(_mix_out_kernel, tm=tm, seq=seq),
        out_shape=jax.ShapeDtypeStruct((T, D_MODEL), F32),
        grid=(T // tm,),
        in_specs=[
            pl.BlockSpec((tm, POOL_WIDTH), lambda t: (t, 0)),
            pl.BlockSpec((HALO, POOL_WIDTH), lambda t: (jnp.maximum(t * (tm // HALO) - 1, 0), 0)),
            pl.BlockSpec((tm, 2 * D_MODEL), lambda t: (t, gate_blk)),
            pl.BlockSpec((tm, ATTN_WIDTH), lambda t: (t, 0)),
            pl.BlockSpec((tm, D_MODEL), lambda t: (t, 0)),
            _const_spec(pw.shape),
            _const_spec(ps.shape),
            _const_spec(wbp.shape),
            _const_spec(wba.shape),
            _const_spec(wo.shape),
            _const_spec(gn.shape),
        ],
        out_specs=pl.BlockSpec((tm, D_MODEL), lambda t: (t, 0)),
        scratch_shapes=[pltpu.VMEM((tm + HALO, POOL_WIDTH), F32)],
        compiler_params=pltpu.CompilerParams(
            dimension_semantics=("parallel",), vmem_limit_bytes=VMEM_LIMIT),
        name="mix_out",
    )(z, z, z, y_attn, x2, pw, ps, wbp, wba, wo, gn)


def _mlp_kernel(h_ref, g1_ref, wu_ref, wd_ref, g2_ref, o_ref, *, ff_chunk):
    h = h_ref[...]
    u = _rms(h, g1_ref[...]).astype(BF16)
    acc = jnp.zeros(h.shape, F32)
    for c in range(D_FF // ff_chunk):
        cs = slice(c * ff_chunk, (c + 1) * ff_chunk)
        a = jnp.maximum(jnp.dot(u, wu_ref[:, cs], preferred_element_type=F32), 0.0)
        acc = acc + jnp.dot((a * a).astype(BF16), wd_ref[cs, :], preferred_element_type=F32)
    o_ref[...] = h + _rms(acc, g2_ref[...])


def _mlp(h, g1, wu, wd, g2, *, tm=1024, ff_chunk=512):
    T = h.shape[0]
    return pl.pallas_call(
        functools.partial(_mlp_kernel, ff_chunk=ff_chunk),
        out_shape=jax.ShapeDtypeStruct((T, D_MODEL), F32),
        grid=(T // tm,),
        in_specs=[
            pl.BlockSpec((tm, D_MODEL), lambda t: (t, 0)),
            _const_spec(g1.shape),
            _const_spec(wu.shape),
            _const_spec(wd.shape),
            _const_spec(g2.shape),
        ],
        out_specs=pl.BlockSpec((tm, D_MODEL), lambda t: (t, 0)),
        compiler_params=pltpu.CompilerParams(
            dimension_semantics=("parallel",), vmem_limit_bytes=VMEM_LIMIT),
        name="mlp",
    )(h, g1, wu, wd, g2)


def kernel(x, norm_mix_pre, w_in, b_gate, pool_w, pool_scale, w_branch_pool, w_branch_attn, w_out,
           norm_mix_post, norm_mlp_pre, w_up, w_down, norm_mlp_post):
    B, S, D = x.shape
    depth = w_in.shape[0]
    kfeat, qconst = _moba_tables(S)
    h = x.reshape(B * S, D)
    for l in range(depth):
        bias = jnp.concatenate(
            [jnp.zeros((IN_WIDTH - b_gate.shape[1],), F32), b_gate[l]])[None, :]
        z = _in_proj(h, norm_mix_pre[l][None, :], w_in[l].astype(BF16), bias)
        y_attn = _moba(z.reshape(B, S, IN_WIDTH), kfeat, qconst).reshape(B * S, ATTN_WIDTH)
        h = _mix_out(z, y_attn, h, pool_w[l].astype(BF16), pool_scale[l][None, :],
                     w_branch_pool[l].astype(BF16), w_branch_attn[l].astype(BF16),
                     w_out[l].astype(BF16), norm_mix_post[l][None, :], seq=S)
        h = _mlp(h, norm_mlp_pre[l][None, :], w_up[l].astype(BF16), w_down[l].astype(BF16),
                 norm_mlp_post[l][None, :])
    return h.reshape(B, S, D)
```

```python
import functools

import jax
import jax.numpy as jnp
import numpy as np
from jax import lax
from jax.experimental import pallas as pl
from jax.experimental.pallas import tpu as pltpu

D_MODEL = 1024
POOL_WIDTH = 512
POOL_WINDOWS = (2, 4, 8, 16)
POOL_GROUP = 128
N_HEADS = 8
HEAD_DIM = 64
ATTN_WIDTH = N_HEADS * HEAD_DIM
MOBA_BLOCK = 256
MOBA_TOPK = 3
D_FF = 4 * D_MODEL
GATE_OFFSET = POOL_WIDTH + 3 * ATTN_WIDTH
IN_WIDTH = GATE_OFFSET + 2 * D_MODEL
EPS = 1e-6
NEG = -1e30

LANES = 128
HEADS_PER_STEP = LANES // HEAD_DIM
N_FEAT = LANES - HEAD_DIM
PV_ROWS = HEAD_DIM + 16
PIPE_DEPTH = 3
PV_CHAINS = 2
HALO = 16
VMEM_LIMIT = 56 * 1024 * 1024

F32 = jnp.float32
BF16 = jnp.bfloat16


def _rms(x, g):
    return x * lax.rsqrt(jnp.mean(x * x, axis=-1, keepdims=True) + EPS) * g


def _const_spec(shape):
    nd = len(shape)
    return pl.BlockSpec(shape, lambda *_: (0,) * nd, pipeline_mode=pl.Buffered(1))


def _in_proj_kernel(x_ref, g_ref, w_ref, b_ref, z_ref, *, col_chunk):
    u = _rms(x_ref[...], g_ref[...]).astype(BF16)
    n_chunks = IN_WIDTH // col_chunk
    first_gate = GATE_OFFSET // col_chunk
    for c in list(range(first_gate, n_chunks)) + list(range(first_gate)):
        cs = slice(c * col_chunk, (c + 1) * col_chunk)
        z = jnp.dot(u, w_ref[:, cs], preferred_element_type=F32)
        if c >= first_gate:
            z = 1.0 / (1.0 + jnp.exp(-(z + b_ref[:, cs])))
        z_ref[:, cs] = z.astype(BF16)


def _in_proj(x2, g, w_bf, bias, *, tm=1024, col_chunk=512):
    T = x2.shape[0]
    assert GATE_OFFSET % col_chunk == 0
    return pl.pallas_call(
        functools.partial(_in_proj_kernel, col_chunk=col_chunk),
        out_shape=jax.ShapeDtypeStruct((T, IN_WIDTH), BF16),
        grid=(T // tm,),
        in_specs=[
            pl.BlockSpec((tm, D_MODEL), lambda t: (t, 0)),
            _const_spec((1, D_MODEL)),
            _const_spec((D_MODEL, IN_WIDTH)),
            _const_spec((1, IN_WIDTH)),
        ],
        out_specs=pl.BlockSpec((tm, IN_WIDTH), lambda t: (t, 0)),
        compiler_params=pltpu.CompilerParams(
            dimension_semantics=("parallel",), vmem_limit_bytes=VMEM_LIMIT),
        name="in_proj",
    )(x2, g, w_bf, bias)


def _moba_kernel(q_ref, k_ref, v_ref, kfeat_ref, qconst_ref, o_ref,
                 kaug_ref, qT_ref, vT_ref, s_ref, oT_ref, *, seq):
    nb = seq // MOBA_BLOCK
    blk = MOBA_BLOCK
    lane = lax.broadcasted_iota(jnp.int32, (1, LANES), 1)
    r8 = lax.broadcasted_iota(jnp.int32, (nb, blk), 0)
    tri = (lax.broadcasted_iota(jnp.int32, (blk, blk), 0)
           <= lax.broadcasted_iota(jnp.int32, (blk, blk), 1))
    ones_rows = (lax.broadcasted_iota(jnp.int32, (PV_ROWS - HEAD_DIM, blk), 0) == 0).astype(F32)

    k2 = k_ref[0]
    kmean = k2.astype(F32).reshape(nb, blk, LANES).sum(axis=1) * (1.0 / blk)
    kmean_h = []
    for hh in range(HEADS_PER_STEP):
        lo = hh * HEAD_DIM
        head_lanes = (lane >= lo) & (lane < lo + HEAD_DIM)
        kaug_ref[hh] = jnp.where(head_lanes, k2, kfeat_ref[...])
        kmean_h.append(jnp.where(head_lanes, kmean, 0.0))
    for j in range(nb):
        rs = slice(j * blk, (j + 1) * blk)
        qT_ref[:, rs] = q_ref[0, rs, :].astype(F32).T * (HEAD_DIM ** -0.5)
        vT = v_ref[0, rs, :].astype(F32).T
        for hh in range(HEADS_PER_STEP):
            vh = vT[hh * HEAD_DIM:(hh + 1) * HEAD_DIM]
            vT_ref[hh, j] = jnp.concatenate([vh, ones_rows], axis=0).astype(BF16)

    def query_operand(i, hh):
        lo = hh * HEAD_DIM
        qc = qconst_ref[hh]
        slope = qc[0:1, :]
        qTi = qT_ref[:, i * blk:(i + 1) * blk]
        if i > MOBA_TOPK:
            gate = jnp.dot(kmean_h[hh], qTi, preferred_element_type=F32,
                           precision=lax.Precision.HIGHEST)
            past = r8 < i
        bias_rows = []
        for j in range(nb):
            if j > i:
                bias_rows.append(jnp.zeros((1, blk), F32))
                continue
            bj = slope * float(blk * (j - i))
            if j < i and i > MOBA_TOPK:
                gj = gate[j:j + 1, :]
                beats = ((gate > gj) | ((gate == gj) & (r8 < j))) & past
                cnt = jnp.sum(beats.astype(F32), axis=0, keepdims=True)
                bj = jnp.where(cnt < float(MOBA_TOPK), bj, NEG)
            bias_rows.append(bj)
        feat = jnp.concatenate(bias_rows + [qc], axis=0)
        qh = qTi[lo:lo + HEAD_DIM]
        parts = [qh, feat] if hh == 0 else [feat, qh]
        return jnp.concatenate(parts, axis=0).astype(BF16)

    def score_block(slot, i, hh, j, qaug, mx):
        s = jnp.dot(kaug_ref[hh, j * blk:(j + 1) * blk, :], qaug,
                    preferred_element_type=F32)
        if j == i:
            s = jnp.where(tri, s, NEG)
        s_ref[slot, j] = s
        cm = s.reshape(blk // 8, 8, blk).max(axis=0)
        return cm if mx is None else jnp.maximum(mx, cm)

    def value_block(slot, hh, j, m, acc):
        p = jnp.exp(s_ref[slot, j] - m).astype(BF16)
        pv = jnp.dot(vT_ref[hh, j], p, preferred_element_type=F32)
        return pv if acc is None else acc + pv

    def emit(i, hh, acc):
        out = acc[:HEAD_DIM] / acc[HEAD_DIM:HEAD_DIM + 1]
        oT_ref[i, hh * HEAD_DIM:(hh + 1) * HEAD_DIM, :] = out
        if hh == HEADS_PER_STEP - 1:
            o_ref[0, i * blk:(i + 1) * blk, :] = oT_ref[i].T.astype(BF16)

    def pass1(slot, i, hh):
        qaug = query_operand(i, hh)
        mx = None
        for j in range(i + 1):
            mx = score_block(slot, i, hh, j, qaug, mx)
        return mx.max(axis=0, keepdims=True)

    def pass2(slot, i, hh, m):
        accs = [None] * PV_CHAINS
        for j in range(i + 1):
            accs[j % PV_CHAINS] = value_block(slot, hh, j, m, accs[j % PV_CHAINS])
        acc = accs[0]
        for a in accs[1:]:
            if a is not None:
                acc = acc + a
        emit(i, hh, acc)

    segs = [(i, hh) for i in range(nb) for hh in range(HEADS_PER_STEP)]
    ms = {}
    for t in range(min(PIPE_DEPTH - 1, len(segs))):
        ms[t] = pass1(t % PIPE_DEPTH, *segs[t])
    for t, (i, hh) in enumerate(segs):
        ahead = t + PIPE_DEPTH - 1
        if ahead < len(segs):
            ms[ahead] = pass1(ahead % PIPE_DEPTH, *segs[ahead])
        pass2(t % PIPE_DEPTH, i, hh, ms.pop(t))


def _alibi_slopes():
    return 2.0 ** (-8.0 * np.arange(1, N_HEADS + 1) / N_HEADS)


def _moba_tables(seq):
    nb = seq // MOBA_BLOCK
    slopes = _alibi_slopes()
    key = np.arange(seq)
    feat = np.zeros((seq, HEAD_DIM), np.float32)
    feat[key, key // MOBA_BLOCK] = 1.0
    feat[:, nb] = key % MOBA_BLOCK
    feat[:, nb + 1] = 1.0
    kfeat = np.concatenate([feat] * HEADS_PER_STEP, axis=1)
    qconst = np.zeros((N_HEADS, N_FEAT - nb, MOBA_BLOCK), np.float32)
    qconst[:, 0, :] = slopes[:, None]
    qconst[:, 1, :] = -slopes[:, None] * np.arange(MOBA_BLOCK)[None, :]
    for t in (kfeat, qconst, slopes[:, None] * MOBA_BLOCK * np.arange(-nb, nb)[None, :]):
        t32 = np.asarray(t, np.float32)
        assert np.array_equal(t32.astype(BF16).astype(np.float32), t32)
    return jnp.asarray(kfeat.astype(BF16)), jnp.asarray(qconst, F32)


def _moba(z3, kfeat, qconst):
    B, S, _ = z3.shape
    nb = S // MOBA_BLOCK
    assert nb <= 8 and S % MOBA_BLOCK == 0 and nb + 2 <= N_FEAT
    n_groups = N_HEADS // HEADS_PER_STEP
    q0 = POOL_WIDTH // LANES
    k0 = q0 + ATTN_WIDTH // LANES
    v0 = k0 + ATTN_WIDTH // LANES
    return pl.pallas_call(
        functools.partial(_moba_kernel, seq=S),
        out_shape=jax.ShapeDtypeStruct((B, S, ATTN_WIDTH), BF16),
        grid=(n_groups, B),
        in_specs=[
            pl.BlockSpec((1, S, LANES), lambda g, b: (b, 0, q0 + g)),
            pl.BlockSpec((1, S, LANES), lambda g, b: (b, 0, k0 + g)),
            pl.BlockSpec((1, S, LANES), lambda g, b: (b, 0, v0 + g)),
            _const_spec((S, LANES)),
            pl.BlockSpec((HEADS_PER_STEP, N_FEAT - nb, MOBA_BLOCK), lambda g, b: (g, 0, 0)),
        ],
        out_specs=pl.BlockSpec((1, S, LANES), lambda g, b: (b, 0, g)),
        scratch_shapes=[
            pltpu.VMEM((HEADS_PER_STEP, S, LANES), BF16),
            pltpu.VMEM((LANES, S), F32),
            pltpu.VMEM((HEADS_PER_STEP, nb, PV_ROWS, MOBA_BLOCK), BF16),
            pltpu.VMEM((PIPE_DEPTH, nb, MOBA_BLOCK, MOBA_BLOCK), F32),
            pltpu.VMEM((nb, LANES, MOBA_BLOCK), F32),
        ],
        compiler_params=pltpu.CompilerParams(
            dimension_semantics=("parallel", "parallel"), vmem_limit_bytes=VMEM_LIMIT),
        name="moba",
    )(z3, z3, z3, kfeat, qconst)


def _mix_out_kernel(p_ref, halo_ref, gate_ref, ya_ref, x_ref, pw_ref, ps_ref, wbp_ref,
                    wba_ref, wo_ref, gn_ref, o_ref, ext_ref, wpool_ref, *, tm, seq):
    t = pl.program_id(0)

    @pl.when(t == 0)
    def _():
        for g in range(len(POOL_WINDOWS)):
            cs = slice(g * POOL_GROUP, (g + 1) * POOL_GROUP)
            wpool_ref[cs, :] = jnp.dot(
                pw_ref[g] * ps_ref[:, cs], wbp_ref[cs, :], preferred_element_type=F32,
                precision=lax.Precision.HIGHEST).astype(BF16)

    pos0 = (t * tm) % seq
    pf = p_ref[...].astype(F32)
    halo = halo_ref[...].astype(F32)
    ext_ref[0:HALO, :] = jnp.where(pos0 == 0, 0.0, halo)
    ext_ref[HALO:, :] = pf
    pos = pos0 + lax.broadcasted_iota(jnp.int32, (tm, 1), 0)
    ys = []
    for g, w in enumerate(POOL_WINDOWS):
        cs = slice(g * POOL_GROUP, (g + 1) * POOL_GROUP)
        pg = pf[:, cs]
        win = pg
        for d in range(1, w):
            win = win + ext_ref[HALO - d:HALO - d + tm, cs]
        count = jnp.minimum(pos + 1, w).astype(F32)
        ys.append((win / count - pg).astype(BF16))
    y_pool = jnp.dot(jnp.concatenate(ys, axis=1), wpool_ref[...], preferred_element_type=F32)
    y_attn = jnp.dot(ya_ref[...], wba_ref[...], preferred_element_type=F32)
    m = gate_ref[:, :D_MODEL].astype(F32) * y_pool + gate_ref[:, D_MODEL:].astype(F32) * y_attn
    mo = jnp.dot(m.astype(BF16), wo_ref[...], preferred_element_type=F32)
    o_ref[...] = x_ref[...] + _rms(mo, gn_ref[...])


def _mix_out(z, y_attn, x2, pw, ps, wbp, wba, wo, gn, *, seq, tm=1024):
    T = x2.shape[0]
    assert seq % tm == 0 and tm % HALO == 0
    gate_blk = (POOL_WIDTH + 3 * ATTN_WIDTH) // (2 * D_MODEL)
    return pl.pallas_call(
        functools.partial(_mix_out_kernel, tm=tm, seq=seq),
        out_shape=jax.ShapeDtypeStruct((T, D_MODEL), F32),
        grid=(T // tm,),
        in_specs=[
            pl.BlockSpec((tm, POOL_WIDTH), lambda t: (t, 0)),
            pl.BlockSpec((HALO, POOL_WIDTH), lambda t: (jnp.maximum(t * (tm // HALO) - 1, 0), 0)),
            pl.BlockSpec((tm, 2 * D_MODEL), lambda t: (t, gate_blk)),
            pl.BlockSpec((tm, ATTN_WIDTH), lambda t: (t, 0)),
            pl.BlockSpec((tm, D_MODEL), lambda t: (t, 0)),
            _const_spec(pw.shape),
            _const_spec(ps.shape),
            _const_spec(wbp.shape),
            _const_spec(wba.shape),
            _const_spec(wo.shape),
            _const_spec(gn.shape),
        ],
        out_specs=pl.BlockSpec((tm, D_MODEL), lambda t: (t, 0)),
        scratch_shapes=[pltpu.VMEM((tm + HALO, POOL_WIDTH), F32),
                        pltpu.VMEM((POOL_WIDTH, D_MODEL), BF16)],
        compiler_params=pltpu.CompilerParams(
            dimension_semantics=("arbitrary",), vmem_limit_bytes=VMEM_LIMIT),
        name="mix_out",
    )(z, z, z, y_attn, x2, pw, ps, wbp, wba, wo, gn)


def _mlp_kernel(h_ref, g1_ref, wu_ref, wd_ref, g2_ref, o_ref, *, ff_chunk):
    h = h_ref[...]
    u = _rms(h, g1_ref[...]).astype(BF16)
    acc = jnp.zeros(h.shape, F32)
    for c in range(D_FF // ff_chunk):
        cs = slice(c * ff_chunk, (c + 1) * ff_chunk)
        a = jnp.maximum(jnp.dot(u, wu_ref[:, cs], preferred_element_type=F32), 0.0)
        acc = acc + jnp.dot((a * a).astype(BF16), wd_ref[cs, :], preferred_element_type=F32)
    o_ref[...] = h + _rms(acc, g2_ref[...])


def _mlp(h, g1, wu, wd, g2, *, tm=1024, ff_chunk=512):
    T = h.shape[0]
    return pl.pallas_call(
        functools.partial(_mlp_kernel, ff_chunk=ff_chunk),
        out_shape=jax.ShapeDtypeStruct((T, D_MODEL), F32),
        grid=(T // tm,),
        in_specs=[
            pl.BlockSpec((tm, D_MODEL), lambda t: (t, 0)),
            _const_spec(g1.shape),
            _const_spec(wu.shape),
            _const_spec(wd.shape),
            _const_spec(g2.shape),
        ],
        out_specs=pl.BlockSpec((tm, D_MODEL), lambda t: (t, 0)),
        compiler_params=pltpu.CompilerParams(
            dimension_semantics=("parallel",), vmem_limit_bytes=VMEM_LIMIT),
        name="mlp",
    )(h, g1, wu, wd, g2)


def kernel(x, norm_mix_pre, w_in, b_gate, pool_w, pool_scale, w_branch_pool, w_branch_attn, w_out,
           norm_mix_post, norm_mlp_pre, w_up, w_down, norm_mlp_post):
    B, S, D = x.shape
    depth = w_in.shape[0]
    kfeat, qconst = _moba_tables(S)
    h = x.reshape(B * S, D)
    for l in range(depth):
        bias = jnp.concatenate(
            [jnp.zeros((IN_WIDTH - b_gate.shape[1],), F32), b_gate[l]])[None, :]
        z = _in_proj(h, norm_mix_pre[l][None, :], w_in[l].astype(BF16), bias)
        y_attn = _moba(z.reshape(B, S, IN_WIDTH), kfeat, qconst).reshape(B * S, ATTN_WIDTH)
        h = _mix_out(z, y_attn, h, pool_w[l], pool_scale[l][None, :],
                     w_branch_pool[l], w_branch_attn[l].astype(BF16),
                     w_out[l].astype(BF16), norm_mix_post[l][None, :], seq=S)
        h = _mlp(h, norm_mlp_pre[l][None, :], w_up[l].astype(BF16), w_down[l].astype(BF16),
                 norm_mlp_post[l][None, :])
    return h.reshape(B, S, D)
```

```python
import functools

import jax
import jax.numpy as jnp
import numpy as np
from jax import lax
from jax.experimental import pallas as pl
from jax.experimental.pallas import tpu as pltpu

D_MODEL = 1024
POOL_WIDTH = 512
POOL_WINDOWS = (2, 4, 8, 16)
POOL_GROUP = 128
N_HEADS = 8
HEAD_DIM = 64
ATTN_WIDTH = N_HEADS * HEAD_DIM
MOBA_BLOCK = 256
MOBA_TOPK = 3
D_FF = 4 * D_MODEL
Z_WIDTH = POOL_WIDTH + 3 * ATTN_WIDTH
GATE_WIDTH = 2 * D_MODEL
EPS = 1e-6
NEG = -1e30

LANES = 128
HEADS_PER_STEP = LANES // HEAD_DIM
N_FEAT = LANES - HEAD_DIM
PV_ROWS = HEAD_DIM + 16
PIPE_DEPTH = 3
PV_CHAINS = 2
HALO = 16
VMEM_LIMIT = 56 * 1024 * 1024

F32 = jnp.float32
BF16 = jnp.bfloat16


def _rms(x, g):
    return x * lax.rsqrt(jnp.mean(x * x, axis=-1, keepdims=True) + EPS) * g


def _const_spec(shape):
    nd = len(shape)
    return pl.BlockSpec(shape, lambda *_: (0,) * nd, pipeline_mode=pl.Buffered(1))


def _in_proj_kernel(x_ref, g_ref, w_ref, z_ref, u_ref, *, col_chunk):
    u = _rms(x_ref[...], g_ref[...]).astype(BF16)
    u_ref[...] = u
    for c in range(Z_WIDTH // col_chunk):
        cs = slice(c * col_chunk, (c + 1) * col_chunk)
        z_ref[:, cs] = jnp.dot(u, w_ref[:, cs], preferred_element_type=F32).astype(BF16)


def _in_proj(x2, g, w_bf, *, tm=1024, col_chunk=512):
    T = x2.shape[0]
    return pl.pallas_call(
        functools.partial(_in_proj_kernel, col_chunk=col_chunk),
        out_shape=(jax.ShapeDtypeStruct((T, Z_WIDTH), BF16),
                   jax.ShapeDtypeStruct((T, D_MODEL), BF16)),
        grid=(T // tm,),
        in_specs=[
            pl.BlockSpec((tm, D_MODEL), lambda t: (t, 0)),
            _const_spec((1, D_MODEL)),
            _const_spec((D_MODEL, Z_WIDTH)),
        ],
        out_specs=(pl.BlockSpec((tm, Z_WIDTH), lambda t: (t, 0)),
                   pl.BlockSpec((tm, D_MODEL), lambda t: (t, 0))),
        compiler_params=pltpu.CompilerParams(
            dimension_semantics=("parallel",), vmem_limit_bytes=VMEM_LIMIT),
        name="in_proj",
    )(x2, g, w_bf)


def _moba_kernel(q_ref, k_ref, v_ref, kfeat_ref, qconst_ref, u_ref, wg_ref, bg_ref,
                 o_ref, gate_ref, kaug_ref, qT_ref, vT_ref, s_ref, oT_ref, *, seq):
    nb = seq // MOBA_BLOCK
    blk = MOBA_BLOCK
    lane = lax.broadcasted_iota(jnp.int32, (1, LANES), 1)
    r8 = lax.broadcasted_iota(jnp.int32, (nb, blk), 0)
    tri = (lax.broadcasted_iota(jnp.int32, (blk, blk), 0)
           <= lax.broadcasted_iota(jnp.int32, (blk, blk), 1))
    ones_rows = (lax.broadcasted_iota(jnp.int32, (PV_ROWS - HEAD_DIM, blk), 0) == 0).astype(F32)

    k2 = k_ref[0]
    kmean = k2.astype(F32).reshape(nb, blk, LANES).sum(axis=1) * (1.0 / blk)
    kmean_h = []
    for hh in range(HEADS_PER_STEP):
        lo = hh * HEAD_DIM
        head_lanes = (lane >= lo) & (lane < lo + HEAD_DIM)
        kaug_ref[hh] = jnp.where(head_lanes, k2, kfeat_ref[...])
        kmean_h.append(jnp.where(head_lanes, kmean, 0.0))
    for j in range(nb):
        rs = slice(j * blk, (j + 1) * blk)
        qT_ref[:, rs] = q_ref[0, rs, :].astype(F32).T * (HEAD_DIM ** -0.5)
        vT = v_ref[0, rs, :].astype(F32).T
        for hh in range(HEADS_PER_STEP):
            vh = vT[hh * HEAD_DIM:(hh + 1) * HEAD_DIM]
            vT_ref[hh, j] = jnp.concatenate([vh, ones_rows], axis=0).astype(BF16)

    def query_operand(i, hh):
        lo = hh * HEAD_DIM
        qc = qconst_ref[hh]
        slope = qc[0:1, :]
        qTi = qT_ref[:, i * blk:(i + 1) * blk]
        if i > MOBA_TOPK:
            gate = jnp.dot(kmean_h[hh], qTi, preferred_element_type=F32,
                           precision=lax.Precision.HIGHEST)
            past = r8 < i
        bias_rows = []
        for j in range(nb):
            if j > i:
                bias_rows.append(jnp.zeros((1, blk), F32))
                continue
            bj = slope * float(blk * (j - i))
            if j < i and i > MOBA_TOPK:
                gj = gate[j:j + 1, :]
                beats = ((gate > gj) | ((gate == gj) & (r8 < j))) & past
                cnt = jnp.sum(beats.astype(F32), axis=0, keepdims=True)
                bj = jnp.where(cnt < float(MOBA_TOPK), bj, NEG)
            bias_rows.append(bj)
        feat = jnp.concatenate(bias_rows + [qc], axis=0)
        qh = qTi[lo:lo + HEAD_DIM]
        parts = [qh, feat] if hh == 0 else [feat, qh]
        return jnp.concatenate(parts, axis=0).astype(BF16)

    def score_block(slot, i, hh, j, qaug, mx):
        s = jnp.dot(kaug_ref[hh, j * blk:(j + 1) * blk, :], qaug,
                    preferred_element_type=F32)
        if j == i:
            s = jnp.where(tri, s, NEG)
        s_ref[slot, j] = s
        cm = s.reshape(blk // 8, 8, blk).max(axis=0)
        return cm if mx is None else jnp.maximum(mx, cm)

    def value_block(slot, hh, j, m, acc):
        p = jnp.exp(s_ref[slot, j] - m).astype(BF16)
        pv = jnp.dot(vT_ref[hh, j], p, preferred_element_type=F32)
        return pv if acc is None else acc + pv

    def emit(i, hh, acc):
        out = acc[:HEAD_DIM] / acc[HEAD_DIM:HEAD_DIM + 1]
        oT_ref[i, hh * HEAD_DIM:(hh + 1) * HEAD_DIM, :] = out
        if hh == HEADS_PER_STEP - 1:
            o_ref[0, i * blk:(i + 1) * blk, :] = oT_ref[i].T.astype(BF16)

    def pass1(slot, i, hh):
        qaug = query_operand(i, hh)
        mx = None
        for j in range(i + 1):
            mx = score_block(slot, i, hh, j, qaug, mx)
        return mx.max(axis=0, keepdims=True)

    def pass2(slot, i, hh, m):
        accs = [None] * PV_CHAINS
        for j in range(i + 1):
            accs[j % PV_CHAINS] = value_block(slot, hh, j, m, accs[j % PV_CHAINS])
        acc = accs[0]
        for a in accs[1:]:
            if a is not None:
                acc = acc + a
        emit(i, hh, acc)

    def gate_piece(t, n_pieces):
        rows = 2 * seq // n_pieces
        half = wg_ref.shape[1] // 2
        rs = slice((t // 2) * rows, (t // 2 + 1) * rows)
        cs = slice((t % 2) * half, (t % 2 + 1) * half)
        g = bg_ref[:, cs]
        for kc in range(0, D_MODEL, 256):
            g = g + jnp.dot(u_ref[0, rs, kc:kc + 256], wg_ref[kc:kc + 256, cs],
                            preferred_element_type=F32)
        gate_ref[0, rs, cs] = g.astype(BF16)

    segs = [(i, hh) for i in range(nb) for hh in range(HEADS_PER_STEP)]
    ms = {}
    for t in range(min(PIPE_DEPTH - 1, len(segs))):
        ms[t] = pass1(t % PIPE_DEPTH, *segs[t])
    for t, (i, hh) in enumerate(segs):
        ahead = t + PIPE_DEPTH - 1
        if ahead < len(segs):
            ms[ahead] = pass1(ahead % PIPE_DEPTH, *segs[ahead])
        pass2(t % PIPE_DEPTH, i, hh, ms.pop(t))
        gate_piece(t, len(segs))


def _alibi_slopes():
    return 2.0 ** (-8.0 * np.arange(1, N_HEADS + 1) / N_HEADS)


def _moba_tables(seq):
    nb = seq // MOBA_BLOCK
    slopes = _alibi_slopes()
    key = np.arange(seq)
    feat = np.zeros((seq, HEAD_DIM), np.float32)
    feat[key, key // MOBA_BLOCK] = 1.0
    feat[:, nb] = key % MOBA_BLOCK
    feat[:, nb + 1] = 1.0
    kfeat = np.concatenate([feat] * HEADS_PER_STEP, axis=1)
    qconst = np.zeros((N_HEADS, N_FEAT - nb, MOBA_BLOCK), np.float32)
    qconst[:, 0, :] = slopes[:, None]
    qconst[:, 1, :] = -slopes[:, None] * np.arange(MOBA_BLOCK)[None, :]
    for t in (kfeat, qconst, slopes[:, None] * MOBA_BLOCK * np.arange(-nb, nb)[None, :]):
        t32 = np.asarray(t, np.float32)
        assert np.array_equal(t32.astype(BF16).astype(np.float32), t32)
    return jnp.asarray(kfeat.astype(BF16)), jnp.asarray(qconst, F32)


def _moba(z3, u3, wg, bg, kfeat, qconst):
    B, S, _ = z3.shape
    nb = S // MOBA_BLOCK
    assert nb <= 8 and S % MOBA_BLOCK == 0 and nb + 2 <= N_FEAT
    n_groups = N_HEADS // HEADS_PER_STEP
    gate_cols = GATE_WIDTH // n_groups
    q0 = POOL_WIDTH // LANES
    k0 = q0 + ATTN_WIDTH // LANES
    v0 = k0 + ATTN_WIDTH // LANES
    return pl.pallas_call(
        functools.partial(_moba_kernel, seq=S),
        out_shape=(jax.ShapeDtypeStruct((B, S, ATTN_WIDTH), BF16),
                   jax.ShapeDtypeStruct((B, S, GATE_WIDTH), BF16)),
        grid=(B, n_groups),
        in_specs=[
            pl.BlockSpec((1, S, LANES), lambda b, g: (b, 0, q0 + g)),
            pl.BlockSpec((1, S, LANES), lambda b, g: (b, 0, k0 + g)),
            pl.BlockSpec((1, S, LANES), lambda b, g: (b, 0, v0 + g)),
            _const_spec((S, LANES)),
            pl.BlockSpec((HEADS_PER_STEP, N_FEAT - nb, MOBA_BLOCK), lambda b, g: (g, 0, 0)),
            pl.BlockSpec((1, S, D_MODEL), lambda b, g: (b, 0, 0)),
            pl.BlockSpec((D_MODEL, gate_cols), lambda b, g: (0, g)),
            pl.BlockSpec((1, gate_cols), lambda b, g: (0, g)),
        ],
        out_specs=(pl.BlockSpec((1, S, LANES), lambda b, g: (b, 0, g)),
                   pl.BlockSpec((1, S, gate_cols), lambda b, g: (b, 0, g))),
        scratch_shapes=[
            pltpu.VMEM((HEADS_PER_STEP, S, LANES), BF16),
            pltpu.VMEM((LANES, S), F32),
            pltpu.VMEM((HEADS_PER_STEP, nb, PV_ROWS, MOBA_BLOCK), BF16),
            pltpu.VMEM((PIPE_DEPTH, nb, MOBA_BLOCK, MOBA_BLOCK), F32),
            pltpu.VMEM((nb, LANES, MOBA_BLOCK), F32),
        ],
        compiler_params=pltpu.CompilerParams(
            dimension_semantics=("parallel", "parallel"), vmem_limit_bytes=VMEM_LIMIT),
        name="moba",
    )(z3, z3, z3, kfeat, qconst, u3, wg, bg)


def _mix_out_kernel(p_ref, halo_ref, gate_ref, ya_ref, x_ref, pw_ref, ps_ref, wbp_ref,
                    wba_ref, wo_ref, gn_ref, o_ref, ext_ref, wpool_ref, *, tm, seq):
    t = pl.program_id(0)

    @pl.when(t == 0)
    def _():
        for g in range(len(POOL_WINDOWS)):
            cs = slice(g * POOL_GROUP, (g + 1) * POOL_GROUP)
            wpool_ref[cs, :] = jnp.dot(
                pw_ref[g] * ps_ref[:, cs], wbp_ref[cs, :], preferred_element_type=F32,
                precision=lax.Precision.HIGHEST).astype(BF16)

    pos0 = (t * tm) % seq
    pf = p_ref[...].astype(F32)
    halo = halo_ref[...].astype(F32)
    ext_ref[0:HALO, :] = jnp.where(pos0 == 0, 0.0, halo)
    ext_ref[HALO:, :] = pf
    pos = pos0 + lax.broadcasted_iota(jnp.int32, (tm, 1), 0)
    ys = []
    for g, w in enumerate(POOL_WINDOWS):
        cs = slice(g * POOL_GROUP, (g + 1) * POOL_GROUP)
        pg = pf[:, cs]
        win = pg
        for d in range(1, w):
            win = win + ext_ref[HALO - d:HALO - d + tm, cs]
        count = jnp.minimum(pos + 1, w).astype(F32)
        ys.append((win / count - pg).astype(BF16))
    y_pool = jnp.dot(jnp.concatenate(ys, axis=1), wpool_ref[...], preferred_element_type=F32)
    y_attn = jnp.dot(ya_ref[...], wba_ref[...], preferred_element_type=F32)
    gp = gate_ref[:, :D_MODEL].astype(F32)
    ga = gate_ref[:, D_MODEL:].astype(F32)
    m = y_pool / (1.0 + jnp.exp(-gp)) + y_attn / (1.0 + jnp.exp(-ga))
    mo = jnp.dot(m.astype(BF16), wo_ref[...], preferred_element_type=F32)
    o_ref[...] = x_ref[...] + _rms(mo, gn_ref[...])


def _mix_out(z, gates, y_attn, x2, pw, ps, wbp, wba, wo, gn, *, seq, tm=1024):
    T = x2.shape[0]
    assert seq % tm == 0 and tm % HALO == 0
    return pl.pallas_call(
        functools.partial(_mix_out_kernel, tm=tm, seq=seq),
        out_shape=jax.ShapeDtypeStruct((T, D_MODEL), F32),
        grid=(T // tm,),
        in_specs=[
            pl.BlockSpec((tm, POOL_WIDTH), lambda t: (t, 0)),
            pl.BlockSpec((HALO, POOL_WIDTH), lambda t: (jnp.maximum(t * (tm // HALO) - 1, 0), 0)),
            pl.BlockSpec((tm, GATE_WIDTH), lambda t: (t, 0)),
            pl.BlockSpec((tm, ATTN_WIDTH), lambda t: (t, 0)),
            pl.BlockSpec((tm, D_MODEL), lambda t: (t, 0)),
            _const_spec(pw.shape),
            _const_spec(ps.shape),
            _const_spec(wbp.shape),
            _const_spec(wba.shape),
            _const_spec(wo.shape),
            _const_spec(gn.shape),
        ],
        out_specs=pl.BlockSpec((tm, D_MODEL), lambda t: (t, 0)),
        scratch_shapes=[pltpu.VMEM((tm + HALO, POOL_WIDTH), F32),
                        pltpu.VMEM((POOL_WIDTH, D_MODEL), BF16)],
        compiler_params=pltpu.CompilerParams(
            dimension_semantics=("arbitrary",), vmem_limit_bytes=VMEM_LIMIT),
        name="mix_out",
    )(z, z, gates, y_attn, x2, pw, ps, wbp, wba, wo, gn)


def _mlp_kernel(h_ref, g1_ref, wu_ref, wd_ref, g2_ref, o_ref, *, ff_chunk):
    h = h_ref[...]
    u = _rms(h, g1_ref[...]).astype(BF16)
    acc = jnp.zeros(h.shape, F32)
    for c in range(D_FF // ff_chunk):
        cs = slice(c * ff_chunk, (c + 1) * ff_chunk)
        a = jnp.maximum(jnp.dot(u, wu_ref[:, cs], preferred_element_type=F32), 0.0)
        acc = acc + jnp.dot((a * a).astype(BF16), wd_ref[cs, :], preferred_element_type=F32)
    o_ref[...] = h + _rms(acc, g2_ref[...])


def _mlp(h, g1, wu, wd, g2, *, tm=1024, ff_chunk=512):
    T = h.shape[0]
    return pl.pallas_call(
        functools.partial(_mlp_kernel, ff_chunk=ff_chunk),
        out_shape=jax.ShapeDtypeStruct((T, D_MODEL), F32),
        grid=(T // tm,),
        in_specs=[
            pl.BlockSpec((tm, D_MODEL), lambda t: (t, 0)),
            _const_spec(g1.shape),
            _const_spec(wu.shape),
            _const_spec(wd.shape),
            _const_spec(g2.shape),
        ],
        out_specs=pl.BlockSpec((tm, D_MODEL), lambda t: (t, 0)),
        compiler_params=pltpu.CompilerParams(
            dimension_semantics=("parallel",), vmem_limit_bytes=VMEM_LIMIT),
        name="mlp",
    )(h, g1, wu, wd, g2)


def kernel(x, norm_mix_pre, w_in, b_gate, pool_w, pool_scale, w_branch_pool, w_branch_attn, w_out,
           norm_mix_post, norm_mlp_pre, w_up, w_down, norm_mlp_post):
    B, S, D = x.shape
    T = B * S
    depth = w_in.shape[0]
    kfeat, qconst = _moba_tables(S)
    h = x.reshape(T, D)
    for l in range(depth):
        w_in_bf = w_in[l].astype(BF16)
        z, u = _in_proj(h, norm_mix_pre[l][None, :], w_in_bf[:, :Z_WIDTH])
        y_attn, gates = _moba(z.reshape(B, S, Z_WIDTH), u.reshape(B, S, D),
                              w_in_bf[:, Z_WIDTH:], b_gate[l][None, :], kfeat, qconst)
        h = _mix_out(z, gates.reshape(T, GATE_WIDTH), y_attn.reshape(T, ATTN_WIDTH), h,
                     pool_w[l], pool_scale[l][None, :], w_branch_pool[l],
                     w_branch_attn[l].astype(BF16), w_out[l].astype(BF16),
                     norm_mix_post[l][None, :], seq=S)
        h = _mlp(h, norm_mlp_pre[l][None, :], w_up[l].astype(BF16), w_down[l].astype(BF16),
                 norm_mlp_post[l][None, :])
    return h.reshape(B, S, D)
```

```python
import functools

import jax
import jax.numpy as jnp
import numpy as np
from jax import lax
from jax.experimental import pallas as pl
from jax.experimental.pallas import tpu as pltpu

D_MODEL = 1024
POOL_WIDTH = 512
POOL_WINDOWS = (2, 4, 8, 16)
POOL_GROUP = 128
N_HEADS = 8
HEAD_DIM = 64
ATTN_WIDTH = N_HEADS * HEAD_DIM
MOBA_BLOCK = 256
MOBA_TOPK = 3
D_FF = 4 * D_MODEL
Z_WIDTH = POOL_WIDTH + 3 * ATTN_WIDTH
GATE_WIDTH = 2 * D_MODEL
EPS = 1e-6
NEG = -1e30

LANES = 128
HEADS_PER_STEP = LANES // HEAD_DIM
N_FEAT = LANES - HEAD_DIM
PV_ROWS = HEAD_DIM + 16
PIPE_DEPTH = 3
PV_CHAINS = 2
HALO = 16
VMEM_LIMIT = 56 * 1024 * 1024

F32 = jnp.float32
BF16 = jnp.bfloat16


def _rms(x, g):
    return x * lax.rsqrt(jnp.mean(x * x, axis=-1, keepdims=True) + EPS) * g


def _const_spec(shape):
    nd = len(shape)
    return pl.BlockSpec(shape, lambda *_: (0,) * nd, pipeline_mode=pl.Buffered(1))


def _in_proj_kernel(x_ref, g_ref, w_ref, z_ref, u_ref, *, col_chunk):
    u = _rms(x_ref[...], g_ref[...]).astype(BF16)
    u_ref[...] = u
    for c in range(Z_WIDTH // col_chunk):
        cs = slice(c * col_chunk, (c + 1) * col_chunk)
        z_ref[:, cs] = jnp.dot(u, w_ref[:, cs], preferred_element_type=F32).astype(BF16)


def _in_proj(x2, g, w_bf, *, tm=1024, col_chunk=512):
    T = x2.shape[0]
    return pl.pallas_call(
        functools.partial(_in_proj_kernel, col_chunk=col_chunk),
        out_shape=(jax.ShapeDtypeStruct((T, Z_WIDTH), BF16),
                   jax.ShapeDtypeStruct((T, D_MODEL), BF16)),
        grid=(T // tm,),
        in_specs=[
            pl.BlockSpec((tm, D_MODEL), lambda t: (t, 0)),
            _const_spec((1, D_MODEL)),
            _const_spec((D_MODEL, Z_WIDTH)),
        ],
        out_specs=(pl.BlockSpec((tm, Z_WIDTH), lambda t: (t, 0)),
                   pl.BlockSpec((tm, D_MODEL), lambda t: (t, 0))),
        compiler_params=pltpu.CompilerParams(
            dimension_semantics=("parallel",), vmem_limit_bytes=VMEM_LIMIT),
        name="in_proj",
    )(x2, g, w_bf)


def _moba_kernel(q_ref, k_ref, v_ref, kfeat_ref, qconst_ref, u_ref, wg_ref, bg_ref,
                 o_ref, gate_ref, kaug_ref, qT_ref, vT_ref, s_ref, oT_ref, *, seq):
    nb = seq // MOBA_BLOCK
    blk = MOBA_BLOCK
    lane = lax.broadcasted_iota(jnp.int32, (1, LANES), 1)
    r8 = lax.broadcasted_iota(jnp.int32, (nb, blk), 0)
    tri = (lax.broadcasted_iota(jnp.int32, (blk, blk), 0)
           <= lax.broadcasted_iota(jnp.int32, (blk, blk), 1))
    ones_rows = (lax.broadcasted_iota(jnp.int32, (PV_ROWS - HEAD_DIM, blk), 0) == 0).astype(F32)

    k2 = k_ref[0]
    kmean = k2.astype(F32).reshape(nb, blk, LANES).sum(axis=1) * (1.0 / blk)
    kmean_h = []
    for hh in range(HEADS_PER_STEP):
        lo = hh * HEAD_DIM
        head_lanes = (lane >= lo) & (lane < lo + HEAD_DIM)
        kaug_ref[hh] = jnp.where(head_lanes, k2, kfeat_ref[...])
        kmean_h.append(jnp.where(head_lanes, kmean, 0.0))
    for j in range(nb):
        rs = slice(j * blk, (j + 1) * blk)
        qT_ref[:, rs] = q_ref[0, rs, :].astype(F32).T * (HEAD_DIM ** -0.5)
        vT = v_ref[0, rs, :].astype(F32).T
        for hh in range(HEADS_PER_STEP):
            vh = vT[hh * HEAD_DIM:(hh + 1) * HEAD_DIM]
            vT_ref[hh, j] = jnp.concatenate([vh, ones_rows], axis=0).astype(BF16)

    def query_operand(i, hh):
        lo = hh * HEAD_DIM
        qc = qconst_ref[hh]
        slope = qc[0:1, :]
        qTi = qT_ref[:, i * blk:(i + 1) * blk]
        if i > MOBA_TOPK:
            gate = jnp.dot(kmean_h[hh], qTi, preferred_element_type=F32,
                           precision=lax.Precision.HIGHEST)
            past = r8 < i
        bias_rows = []
        for j in range(nb):
            if j > i:
                bias_rows.append(jnp.zeros((1, blk), F32))
                continue
            bj = slope * float(blk * (j - i))
            if j < i and i > MOBA_TOPK:
                gj = gate[j:j + 1, :]
                beats = ((gate > gj) | ((gate == gj) & (r8 < j))) & past
                cnt = jnp.sum(beats.astype(F32), axis=0, keepdims=True)
                bj = jnp.where(cnt < float(MOBA_TOPK), bj, NEG)
            bias_rows.append(bj)
        feat = jnp.concatenate(bias_rows + [qc], axis=0)
        qh = qTi[lo:lo + HEAD_DIM]
        parts = [qh, feat] if hh == 0 else [feat, qh]
        return jnp.concatenate(parts, axis=0).astype(BF16)

    def score_block(slot, i, hh, j, qaug, mx):
        s = jnp.dot(kaug_ref[hh, j * blk:(j + 1) * blk, :], qaug,
                    preferred_element_type=F32)
        if j == i:
            s = jnp.where(tri, s, NEG)
        s_ref[slot, j] = s
        cm = s.reshape(blk // 8, 8, blk).max(axis=0)
        return cm if mx is None else jnp.maximum(mx, cm)

    def value_block(slot, hh, j, m, acc):
        p = jnp.exp(s_ref[slot, j] - m).astype(BF16)
        pv = jnp.dot(vT_ref[hh, j], p, preferred_element_type=F32)
        return pv if acc is None else acc + pv

    def emit(i, hh, acc):
        out = acc[:HEAD_DIM] / acc[HEAD_DIM:HEAD_DIM + 1]
        oT_ref[i, hh * HEAD_DIM:(hh + 1) * HEAD_DIM, :] = out
        if hh == HEADS_PER_STEP - 1:
            o_ref[0, i * blk:(i + 1) * blk, :] = oT_ref[i].T.astype(BF16)

    def pass1(slot, i, hh):
        qaug = query_operand(i, hh)
        mx = None
        for j in range(i + 1):
            mx = score_block(slot, i, hh, j, qaug, mx)
        return mx.max(axis=0, keepdims=True)

    def pass2(slot, i, hh, m):
        accs = [None] * PV_CHAINS
        for j in range(i + 1):
            accs[j % PV_CHAINS] = value_block(slot, hh, j, m, accs[j % PV_CHAINS])
        acc = accs[0]
        for a in accs[1:]:
            if a is not None:
                acc = acc + a
        emit(i, hh, acc)

    def gate_piece(t, n_pieces):
        rows = 2 * seq // n_pieces
        half = wg_ref.shape[1] // 2
        rs = slice((t // 2) * rows, (t // 2 + 1) * rows)
        cs = slice((t % 2) * half, (t % 2 + 1) * half)
        g = bg_ref[:, cs]
        for kc in range(0, D_MODEL, 256):
            g = g + jnp.dot(u_ref[0, rs, kc:kc + 256], wg_ref[kc:kc + 256, cs],
                            preferred_element_type=F32)
        gate_ref[0, rs, cs] = g.astype(BF16)

    segs = [(i, hh) for i in range(nb) for hh in range(HEADS_PER_STEP)]
    ms = {}
    for t in range(min(PIPE_DEPTH - 1, len(segs))):
        ms[t] = pass1(t % PIPE_DEPTH, *segs[t])
    for t, (i, hh) in enumerate(segs):
        ahead = t + PIPE_DEPTH - 1
        if ahead < len(segs):
            ms[ahead] = pass1(ahead % PIPE_DEPTH, *segs[ahead])
        pass2(t % PIPE_DEPTH, i, hh, ms.pop(t))
        gate_piece(t, len(segs))


def _alibi_slopes():
    return 2.0 ** (-8.0 * np.arange(1, N_HEADS + 1) / N_HEADS)


def _moba_tables(seq):
    nb = seq // MOBA_BLOCK
    slopes = _alibi_slopes()
    key = np.arange(seq)
    feat = np.zeros((seq, HEAD_DIM), np.float32)
    feat[key, key // MOBA_BLOCK] = 1.0
    feat[:, nb] = key % MOBA_BLOCK
    feat[:, nb + 1] = 1.0
    kfeat = np.concatenate([feat] * HEADS_PER_STEP, axis=1)
    qconst = np.zeros((N_HEADS, N_FEAT - nb, MOBA_BLOCK), np.float32)
    qconst[:, 0, :] = slopes[:, None]
    qconst[:, 1, :] = -slopes[:, None] * np.arange(MOBA_BLOCK)[None, :]
    for t in (kfeat, qconst, slopes[:, None] * MOBA_BLOCK * np.arange(-nb, nb)[None, :]):
        t32 = np.asarray(t, np.float32)
        assert np.array_equal(t32.astype(BF16).astype(np.float32), t32)
    return jnp.asarray(kfeat.astype(BF16)), jnp.asarray(qconst, F32)


def _moba(z3, u3, wg, bg, kfeat, qconst):
    B, S, _ = z3.shape
    nb = S // MOBA_BLOCK
    assert nb <= 8 and S % MOBA_BLOCK == 0 and nb + 2 <= N_FEAT
    n_groups = N_HEADS // HEADS_PER_STEP
    gate_cols = GATE_WIDTH // n_groups
    q0 = POOL_WIDTH // LANES
    k0 = q0 + ATTN_WIDTH // LANES
    v0 = k0 + ATTN_WIDTH // LANES
    return pl.pallas_call(
        functools.partial(_moba_kernel, seq=S),
        out_shape=(jax.ShapeDtypeStruct((B, S, ATTN_WIDTH), BF16),
                   jax.ShapeDtypeStruct((B, S, GATE_WIDTH), BF16)),
        grid=(B, n_groups),
        in_specs=[
            pl.BlockSpec((1, S, LANES), lambda b, g: (b, 0, q0 + g)),
            pl.BlockSpec((1, S, LANES), lambda b, g: (b, 0, k0 + g)),
            pl.BlockSpec((1, S, LANES), lambda b, g: (b, 0, v0 + g)),
            _const_spec((S, LANES)),
            pl.BlockSpec((HEADS_PER_STEP, N_FEAT - nb, MOBA_BLOCK), lambda b, g: (g, 0, 0)),
            pl.BlockSpec((1, S, D_MODEL), lambda b, g: (b, 0, 0)),
            pl.BlockSpec((D_MODEL, gate_cols), lambda b, g: (0, g)),
            pl.BlockSpec((1, gate_cols), lambda b, g: (0, g)),
        ],
        out_specs=(pl.BlockSpec((1, S, LANES), lambda b, g: (b, 0, g)),
                   pl.BlockSpec((1, S, gate_cols), lambda b, g: (b, 0, g))),
        scratch_shapes=[
            pltpu.VMEM((HEADS_PER_STEP, S, LANES), BF16),
            pltpu.VMEM((LANES, S), F32),
            pltpu.VMEM((HEADS_PER_STEP, nb, PV_ROWS, MOBA_BLOCK), BF16),
            pltpu.VMEM((PIPE_DEPTH, nb, MOBA_BLOCK, MOBA_BLOCK), F32),
            pltpu.VMEM((nb, LANES, MOBA_BLOCK), F32),
        ],
        compiler_params=pltpu.CompilerParams(
            dimension_semantics=("parallel", "parallel"), vmem_limit_bytes=VMEM_LIMIT),
        name="moba",
    )(z3, z3, z3, kfeat, qconst, u3, wg, bg)


def _mix_out_kernel(p_ref, halo_ref, gate_ref, ya_ref, x_ref, pw_ref, ps_ref, wbp_ref,
                    wba_ref, wo_ref, gn_ref, o_ref, ext_ref, wpool_ref, *, tm, seq, row_chunk):
    t = pl.program_id(0)

    @pl.when(t == 0)
    def _():
        for g in range(len(POOL_WINDOWS)):
            cs = slice(g * POOL_GROUP, (g + 1) * POOL_GROUP)
            wpool_ref[cs, :] = jnp.dot(
                pw_ref[g] * ps_ref[:, cs], wbp_ref[cs, :], preferred_element_type=F32,
                precision=lax.Precision.HIGHEST).astype(BF16)

    pos0 = (t * tm) % seq
    pf = p_ref[...].astype(F32)
    halo = halo_ref[...].astype(F32)
    ext_ref[0:HALO, :] = jnp.where(pos0 == 0, 0.0, halo)
    ext_ref[HALO:, :] = pf
    for r0 in range(0, tm, row_chunk):
        rs = slice(r0, r0 + row_chunk)
        pos = pos0 + r0 + lax.broadcasted_iota(jnp.int32, (row_chunk, 1), 0)
        ys = []
        for g, w in enumerate(POOL_WINDOWS):
            cs = slice(g * POOL_GROUP, (g + 1) * POOL_GROUP)
            win = ext_ref[r0:r0 + HALO + row_chunk, cs]
            pg = win[HALO:]
            d = 1
            while d < w:
                shifted = jnp.concatenate([jnp.zeros((d, POOL_GROUP), F32), win[:-d]], axis=0)
                win = win + shifted
                d *= 2
            count = jnp.minimum(pos + 1, w).astype(F32)
            ys.append((win[HALO:] / count - pg).astype(BF16))
        y_pool = jnp.dot(jnp.concatenate(ys, axis=1), wpool_ref[...], preferred_element_type=F32)
        y_attn = jnp.dot(ya_ref[rs, :], wba_ref[...], preferred_element_type=F32)
        gp = gate_ref[rs, :D_MODEL].astype(F32)
        ga = gate_ref[rs, D_MODEL:].astype(F32)
        m = y_pool / (1.0 + jnp.exp(-gp)) + y_attn / (1.0 + jnp.exp(-ga))
        mo = jnp.dot(m.astype(BF16), wo_ref[...], preferred_element_type=F32)
        o_ref[rs, :] = x_ref[rs, :] + _rms(mo, gn_ref[...])


def _mix_out(z, gates, y_attn, x2, pw, ps, wbp, wba, wo, gn, *, seq, tm=1024, row_chunk=512):
    T = x2.shape[0]
    assert seq % tm == 0 and tm % HALO == 0 and tm % row_chunk == 0
    return pl.pallas_call(
        functools.partial(_mix_out_kernel, tm=tm, seq=seq, row_chunk=row_chunk),
        out_shape=jax.ShapeDtypeStruct((T, D_MODEL), F32),
        grid=(T // tm,),
        in_specs=[
            pl.BlockSpec((tm, POOL_WIDTH), lambda t: (t, 0)),
            pl.BlockSpec((HALO, POOL_WIDTH), lambda t: (jnp.maximum(t * (tm // HALO) - 1, 0), 0)),
            pl.BlockSpec((tm, GATE_WIDTH), lambda t: (t, 0)),
            pl.BlockSpec((tm, ATTN_WIDTH), lambda t: (t, 0)),
            pl.BlockSpec((tm, D_MODEL), lambda t: (t, 0)),
            _const_spec(pw.shape),
            _const_spec(ps.shape),
            _const_spec(wbp.shape),
            _const_spec(wba.shape),
            _const_spec(wo.shape),
            _const_spec(gn.shape),
        ],
        out_specs=pl.BlockSpec((tm, D_MODEL), lambda t: (t, 0)),
        scratch_shapes=[pltpu.VMEM((tm + HALO, POOL_WIDTH), F32),
                        pltpu.VMEM((POOL_WIDTH, D_MODEL), BF16)],
        compiler_params=pltpu.CompilerParams(
            dimension_semantics=("arbitrary",), vmem_limit_bytes=VMEM_LIMIT),
        name="mix_out",
    )(z, z, gates, y_attn, x2, pw, ps, wbp, wba, wo, gn)


def _mlp_kernel(h_ref, g1_ref, wu_ref, wd_ref, g2_ref, o_ref, *, ff_chunk):
    h = h_ref[...]
    u = _rms(h, g1_ref[...]).astype(BF16)
    acc = jnp.zeros(h.shape, F32)
    for c in range(D_FF // ff_chunk):
        cs = slice(c * ff_chunk, (c + 1) * ff_chunk)
        a = jnp.maximum(jnp.dot(u, wu_ref[:, cs], preferred_element_type=F32), 0.0)
        acc = acc + jnp.dot((a * a).astype(BF16), wd_ref[cs, :], preferred_element_type=F32)
    o_ref[...] = h + _rms(acc, g2_ref[...])


def _mlp(h, g1, wu, wd, g2, *, tm=1024, ff_chunk=512):
    T = h.shape[0]
    return pl.pallas_call(
        functools.partial(_mlp_kernel, ff_chunk=ff_chunk),
        out_shape=jax.ShapeDtypeStruct((T, D_MODEL), F32),
        grid=(T // tm,),
        in_specs=[
            pl.BlockSpec((tm, D_MODEL), lambda t: (t, 0)),
            _const_spec(g1.shape),
            _const_spec(wu.shape),
            _const_spec(wd.shape),
            _const_spec(g2.shape),
        ],
        out_specs=pl.BlockSpec((tm, D_MODEL), lambda t: (t, 0)),
        compiler_params=pltpu.CompilerParams(
            dimension_semantics=("parallel",), vmem_limit_bytes=VMEM_LIMIT),
        name="mlp",
    )(h, g1, wu, wd, g2)


def kernel(x, norm_mix_pre, w_in, b_gate, pool_w, pool_scale, w_branch_pool, w_branch_attn, w_out,
           norm_mix_post, norm_mlp_pre, w_up, w_down, norm_mlp_post):
    B, S, D = x.shape
    T = B * S
    depth = w_in.shape[0]
    kfeat, qconst = _moba_tables(S)
    h = x.reshape(T, D)
    for l in range(depth):
        w_in_bf = w_in[l].astype(BF16)
        z, u = _in_proj(h, norm_mix_pre[l][None, :], w_in_bf[:, :Z_WIDTH])
        y_attn, gates = _moba(z.reshape(B, S, Z_WIDTH), u.reshape(B, S, D),
                              w_in_bf[:, Z_WIDTH:], b_gate[l][None, :], kfeat, qconst)
        h = _mix_out(z, gates.reshape(T, GATE_WIDTH), y_attn.reshape(T, ATTN_WIDTH), h,
                     pool_w[l], pool_scale[l][None, :], w_branch_pool[l],
                     w_branch_attn[l].astype(BF16), w_out[l].astype(BF16),
                     norm_mix_post[l][None, :], seq=S)
        h = _mlp(h, norm_mlp_pre[l][None, :], w_up[l].astype(BF16), w_down[l].astype(BF16),
                 norm_mlp_post[l][None, :])
    return h.reshape(B, S, D)
```
